```python
import math
import jax, jax.numpy as jnp
from jax import lax
import numpy as np

D_MODEL = 4096
BATCH = 4
SEQ = 2048
DEPTH = 2
DEC_BATCH = 8
DEC_SEQ = 1
PAST_LEN = 16384
PAGE_SIZE = 128

N_HEADS = 8
HEAD_DIM = 128
VAL_DIM = 2 * HEAD_DIM
ATT_WIDTH = N_HEADS * 2 * HEAD_DIM
LRU_WIDTH = D_MODEL // 2
LRU_BLOCKS = 16
LRU_BLOCK = LRU_WIDTH // LRU_BLOCKS
CONV_WIDTH = 4
LRU_C = 8.0
D_FF = 11008
N_EXPERTS = 8
TOP_K = 2
D_FF_EXPERT = 14336
N_BUCKETS = 32
MAX_DISTANCE = 128
Q_BLOCK = 128
N_DENSE = (DEPTH + 1) // 2
N_MOE = DEPTH // 2
IN_WIDTH = 3 * ATT_WIDTH + 2 * LRU_WIDTH
EPS = 1e-6

kernel_name = 'diffattn_rglru_gated_hybrid_step'


def rms_norm(x, g):
    xf = x.astype(jnp.float32)
    y = xf * lax.rsqrt(jnp.mean(xf * xf, axis=-1, keepdims=True) + EPS)
    return (y * g.astype(jnp.float32)).astype(x.dtype)


def rel_bucket(q_pos, k_pos):
    n = jnp.maximum(q_pos[:, None] - k_pos[None, :], 0)
    max_exact = N_BUCKETS // 2
    nf = jnp.maximum(n, 1).astype(jnp.float32)
    large = max_exact + (jnp.log(nf / max_exact) / math.log(MAX_DISTANCE / max_exact) * (N_BUCKETS - max_exact)).astype(jnp.int32)
    large = jnp.minimum(large, N_BUCKETS - 1)
    return jnp.where(n < max_exact, n, large)


def diff_attention(q, k, v, q_pos, k_pos, lam, rel_table):
    s = jnp.einsum('bqhmd,bkhmd->bhmqk', q, k).astype(jnp.float32) * (HEAD_DIM ** -0.5)
    bias = rel_table[rel_bucket(q_pos, k_pos)].astype(jnp.float32)
    s = s + jnp.transpose(bias, (2, 0, 1))[None, :, None]
    causal = k_pos[None, :] <= q_pos[:, None]
    s = jnp.where(causal, s, -jnp.inf)
    p = jax.nn.softmax(s, axis=-1)
    a = p[:, :, 0] - lam * p[:, :, 1]
    return jnp.einsum('bhqk,bkhe->bqhe', a.astype(v.dtype), v)


def prompt_attention(q, k, v, lam, rel_table):
    B, S = q.shape[:2]
    nb = S // Q_BLOCK
    qb = jnp.moveaxis(q.reshape(B, nb, Q_BLOCK, N_HEADS, 2, HEAD_DIM), 1, 0)
    k_pos = jnp.arange(S)

    def one_block(args):
        q_i, i = args
        q_pos = i * Q_BLOCK + jnp.arange(Q_BLOCK)
        return diff_attention(q_i, k, v, q_pos, k_pos, lam, rel_table)

    out = lax.map(one_block, (qb, jnp.arange(nb)))
    return jnp.moveaxis(out, 0, 1).reshape(B, S, N_HEADS, VAL_DIM)


def sample_attention(q, k_new, v_new, cache_k, cache_v, layer, page_table, lam, rel_table):
    T = q.shape[1]
    past = page_table.shape[1] * PAGE_SIZE
    q_pos = past + jnp.arange(T)
    k_pos = jnp.arange(past + T)

    def one_seq(args):
        q_b, k_b, v_b, pages = args
        kp = cache_k[layer, pages].reshape(past, N_HEADS, 2, HEAD_DIM).astype(k_b.dtype)
        vp = cache_v[layer, pages].reshape(past, N_HEADS, VAL_DIM).astype(v_b.dtype)
        kk = jnp.concatenate([kp, k_b], axis=0)[None]
        vv = jnp.concatenate([vp, v_b], axis=0)[None]
        return diff_attention(q_b[None], kk, vv, q_pos, k_pos, lam, rel_table)[0]

    return lax.map(one_seq, (q, k_new, v_new, page_table))


def conv_rglru(rx, rg, conv_prev, h_prev, conv_w, conv_b, w_r, b_r, w_i, b_i, lam_lru):
    B, S, W = rx.shape
    xx = jnp.concatenate([conv_prev.astype(rx.dtype), rx], axis=1)
    xc = sum(xx[:, j:j + S] * conv_w[j] for j in range(CONV_WIDTH)) + conv_b
    new_conv = xx[:, S:]
    xb = xc.reshape(B, S, LRU_BLOCKS, LRU_BLOCK)
    r = jax.nn.sigmoid(jnp.einsum('bsnc,ncd->bsnd', xb, w_r).reshape(B, S, W) + b_r)
    i = jax.nn.sigmoid(jnp.einsum('bsnc,ncd->bsnd', xb, w_i).reshape(B, S, W) + b_i)
    log_a = -LRU_C * r.astype(jnp.float32) * jax.nn.softplus(-lam_lru.astype(jnp.float32))
    a = jnp.exp(log_a)
    bx = jnp.sqrt(-jnp.expm1(2.0 * log_a)) * (i * xc).astype(jnp.float32)

    def step(h, ab):
        a_t, b_t = ab
        h = a_t * h + b_t
        return h, h

    h_last, hs = lax.scan(step, h_prev.astype(jnp.float32), (jnp.swapaxes(a, 0, 1), jnp.swapaxes(bx, 0, 1)))
    y = jnp.swapaxes(hs, 0, 1).astype(rx.dtype) * jax.nn.gelu(rg)
    return y, new_conv, h_last.astype(h_prev.dtype)


def swiglu(u, w1, w3, w2):
    return (jax.nn.silu(u @ w1) * (u @ w3)) @ w2


def moe_swiglu(u, w_router, b_router, w1, w3, w2):
    logits = (u @ w_router).astype(jnp.float32) + b_router.astype(jnp.float32)
    top_val, top_idx = lax.top_k(logits, TOP_K)
    top_w = jax.nn.softmax(top_val, axis=-1)
    gates = jnp.einsum('bsk,bske->bse', top_w, jax.nn.one_hot(top_idx, N_EXPERTS, dtype=jnp.float32))

    def add_expert(acc, xs):
        e_w1, e_w3, e_w2, g = xs
        return acc + g[..., None].astype(u.dtype) * swiglu(u, e_w1, e_w3, e_w2), None

    out, _ = lax.scan(add_expert, jnp.zeros_like(u), (w1, w3, w2, jnp.moveaxis(gates, -1, 0)))
    return out


def setup_inputs(seed: int = 0) -> dict:
    key = jax.random.key(seed)
    keys = list(jax.random.split(key, 48))
    f32 = jnp.float32

    def nk():
        return keys.pop()

    def nrm(shape, scale):
        return jax.random.normal(nk(), shape, f32) * scale

    n_pages = PAST_LEN // PAGE_SIZE
    n_used = DEC_BATCH * n_pages
    n_pool = n_used + (n_used + 3) // 4
    D = D_MODEL
    u = jax.random.uniform(nk(), (DEPTH, LRU_WIDTH), f32, minval=0.9, maxval=0.999)
    s = u ** (1.0 / LRU_C)
    lru_lambda = jnp.log(s) - jnp.log1p(-s)
    page_table = jax.random.permutation(nk(), n_pool)[:n_used].reshape(DEC_BATCH, n_pages).astype(jnp.int32)
    return {
        'x_prompt': nrm((BATCH, SEQ, D), 1.0),
        'x_sample': nrm((DEC_BATCH, DEC_SEQ, D), 1.0),
        'c_prompt': nrm((BATCH, D), 1.0),
        'c_sample': nrm((DEC_BATCH, D), 1.0),
        'cache_k': nrm((DEPTH, n_pool, PAGE_SIZE, N_HEADS, 2 * HEAD_DIM), 1.0),
        'cache_v': nrm((DEPTH, n_pool, PAGE_SIZE, N_HEADS, VAL_DIM), 1.0),
        'state_conv': nrm((DEPTH, DEC_BATCH, CONV_WIDTH - 1, LRU_WIDTH), 1.0),
        'state_h': nrm((DEPTH, DEC_BATCH, LRU_WIDTH), 0.5),
        'page_table': page_table,
        'rel_bias': nrm((N_BUCKETS, N_HEADS), 0.5),
        'ada_w': nrm((DEPTH, D, 6 * D), 0.5 * D ** -0.5),
        'ada_b': nrm((DEPTH, 6 * D), 0.02),
        'norm1_g': 1.0 + nrm((DEPTH, D), 0.02),
        'w_in': nrm((DEPTH, D, IN_WIDTH), D ** -0.5),
        'q_norm_g': 1.0 + nrm((DEPTH, HEAD_DIM), 0.02),
        'k_norm_g': 1.0 + nrm((DEPTH, HEAD_DIM), 0.02),
        'lam_q1': nrm((DEPTH, HEAD_DIM), 0.1),
        'lam_k1': nrm((DEPTH, HEAD_DIM), 0.1),
        'lam_q2': nrm((DEPTH, HEAD_DIM), 0.1),
        'lam_k2': nrm((DEPTH, HEAD_DIM), 0.1),
        'subln_g': 1.0 + nrm((DEPTH, VAL_DIM), 0.02),
        'conv_w': nrm((DEPTH, CONV_WIDTH, LRU_WIDTH), CONV_WIDTH ** -0.5),
        'conv_b': nrm((DEPTH, LRU_WIDTH), 0.02),
        'lru_wr': nrm((DEPTH, LRU_BLOCKS, LRU_BLOCK, LRU_BLOCK), LRU_BLOCK ** -0.5),
        'lru_br': nrm((DEPTH, LRU_WIDTH), 0.02),
        'lru_wi': nrm((DEPTH, LRU_BLOCKS, LRU_BLOCK, LRU_BLOCK), LRU_BLOCK ** -0.5),
        'lru_bi': nrm((DEPTH, LRU_WIDTH), 0.02),
        'lru_lambda': lru_lambda,
        'w_att': nrm((DEPTH, N_HEADS * VAL_DIM, D), (N_HEADS * VAL_DIM) ** -0.5),
        'w_lru': nrm((DEPTH, LRU_WIDTH, D), LRU_WIDTH ** -0.5),
        'gate_w': nrm((DEPTH, D, 2 * D), D ** -0.5),
        'gate_b': nrm((DEPTH, 2 * D), 0.02),
        'w_o': nrm((DEPTH, D, D), D ** -0.5),
        'norm2_g': 1.0 + nrm((DEPTH, D), 0.02),
        'ffn_w1': nrm((N_DENSE, D, D_FF), D ** -0.5),
        'ffn_w3': nrm((N_DENSE, D, D_FF), D ** -0.5),
        'ffn_w2': nrm((N_DENSE, D_FF, D), D_FF ** -0.5),
        'router_w': nrm((N_MOE, D, N_EXPERTS), D ** -0.5),
        'router_b': nrm((N_MOE, N_EXPERTS), 0.01),
        'moe_w1': nrm((N_MOE, N_EXPERTS, D, D_FF_EXPERT), D ** -0.5),
        'moe_w3': nrm((N_MOE, N_EXPERTS, D, D_FF_EXPERT), D ** -0.5),
        'moe_w2': nrm((N_MOE, N_EXPERTS, D_FF_EXPERT, D), D_FF_EXPERT ** -0.5),
    }


def reference(x_prompt, x_sample, c_prompt, c_sample, cache_k, cache_v, state_conv, state_h, page_table,
              rel_bias, ada_w, ada_b, norm1_g, w_in, q_norm_g, k_norm_g, lam_q1, lam_k1, lam_q2, lam_k2,
              subln_g, conv_w, conv_b, lru_wr, lru_br, lru_wi, lru_bi, lru_lambda, w_att, w_lru, gate_w, gate_b,
              w_o, norm2_g, ffn_w1, ffn_w3, ffn_w2, router_w, router_b, moe_w1, moe_w3, moe_w2):
    f32 = jnp.float32

    def run_layer(l, x, c, attend, conv_prev, h_prev):
        B, S = x.shape[:2]
        mod = (jax.nn.silu(c) @ ada_w[l] + ada_b[l])[:, None, :]
        sh1, sc1, g1, sh2, sc2, g2 = jnp.split(mod, 6, axis=-1)
        u = rms_norm(x, norm1_g[l]) * (1 + sc1) + sh1
        z = u @ w_in[l]
        q, k, v, rx, rg = jnp.split(z, [ATT_WIDTH, 2 * ATT_WIDTH, 3 * ATT_WIDTH, 3 * ATT_WIDTH + LRU_WIDTH], axis=-1)
        q = rms_norm(q.reshape(B, S, N_HEADS, 2, HEAD_DIM), q_norm_g[l])
        k = rms_norm(k.reshape(B, S, N_HEADS, 2, HEAD_DIM), k_norm_g[l])
        v = v.reshape(B, S, N_HEADS, VAL_DIM)
        lam_init = 0.8 - 0.6 * math.exp(-0.3 * l)
        lam = (jnp.exp(jnp.sum(lam_q1[l].astype(f32) * lam_k1[l].astype(f32)))
               - jnp.exp(jnp.sum(lam_q2[l].astype(f32) * lam_k2[l].astype(f32))) + lam_init)
        o = attend(q, k, v, lam)
        o = (rms_norm(o, subln_g[l]) * (1 - lam_init)).reshape(B, S, N_HEADS * VAL_DIM)
        y_lru, conv_new, h_new = conv_rglru(rx, rg, conv_prev, h_prev, conv_w[l], conv_b[l], lru_wr[l], lru_br[l],
                                            lru_wi[l], lru_bi[l], lru_lambda[l])
        ga, gb = jnp.split(jax.nn.sigmoid(u @ gate_w[l] + gate_b[l]), 2, axis=-1)
        x = x + g1 * ((ga * (o @ w_att[l]) + gb * (y_lru @ w_lru[l])) @ w_o[l])
        u2 = rms_norm(x, norm2_g[l]) * (1 + sc2) + sh2
        if l % 2 == 0:
            f = swiglu(u2, ffn_w1[l // 2], ffn_w3[l // 2], ffn_w2[l // 2])
        else:
            m = l // 2
            f = moe_swiglu(u2, router_w[m], router_b[m], moe_w1[m], moe_w3[m], moe_w2[m])
        x = x + g2 * f
        return x, k.reshape(B, S, N_HEADS, 2 * HEAD_DIM), v, conv_new, h_new

    zeros_conv = jnp.zeros((x_prompt.shape[0], CONV_WIDTH - 1, LRU_WIDTH), x_prompt.dtype)
    zeros_h = jnp.zeros((x_prompt.shape[0], LRU_WIDTH), state_h.dtype)

    xp, xs = x_prompt, x_sample
    kp_l, vp_l, cp_l, hp_l = [], [], [], []
    ks_l, vs_l, cs_l, hs_l = [], [], [], []
    for l in range(DEPTH):
        xp, kp, vp, cp, hp = run_layer(
            l, xp, c_prompt, lambda q, k, v, lam: prompt_attention(q, k, v, lam, rel_bias), zeros_conv, zeros_h)
        xs, ks, vs, cs, hs = run_layer(
            l, xs, c_sample,
            lambda q, k, v, lam, l=l: sample_attention(q, k, v, cache_k, cache_v, l, page_table, lam, rel_bias),
            state_conv[l], state_h[l])
        kp_l.append(kp); vp_l.append(vp); cp_l.append(cp); hp_l.append(hp)
        ks_l.append(ks); vs_l.append(vs); cs_l.append(cs); hs_l.append(hs)

    return (xp, xs, jnp.stack(kp_l), jnp.stack(vp_l), jnp.stack(cp_l), jnp.stack(hp_l),
            jnp.stack(ks_l), jnp.stack(vs_l), jnp.stack(cs_l), jnp.stack(hs_l))
```

```python
import functools
import math

import numpy as np
import jax
import jax.numpy as jnp
from jax import lax
from jax.experimental import pallas as pl
from jax.experimental.pallas import tpu as pltpu

_F32 = jnp.float32
_BF16 = jnp.bfloat16

_EPS = 1e-6
_LRU_C = 8.0
_CONV_WIDTH = 4
_N_BUCKETS = 32
_MAX_DISTANCE = 128
_TOP_K = 2

_LANES = 128
_SUBLANES = 8
_VMEM_CAP_BYTES = 60 * 1024 * 1024
_VMEM_MIN_BYTES = 32 * 1024 * 1024

_NEG_INF = float("-inf")


def _vmem_limit(est_bytes):
    return int(min(_VMEM_CAP_BYTES, max(_VMEM_MIN_BYTES, est_bytes)))


def _nbytes(shape, dtype):
    return int(np.prod(shape)) * jnp.dtype(dtype).itemsize


def _pick(n, pref):
    t = pref
    while t >= _LANES:
        if n % t == 0:
            return t
        t //= 2
    return n


def _mm_body(*refs, n_a, w_a, n_ex, n_out, nk, a_fn, epilogue):
    n_w = len(w_a)
    a_refs = refs[:n_a]
    w_refs = refs[n_a:n_a + n_w]
    ex_refs = refs[n_a + n_w:n_a + n_w + n_ex]
    out_refs = refs[n_a + n_w + n_ex:n_a + n_w + n_ex + n_out]
    acc_refs = refs[n_a + n_w + n_ex + n_out:]
    a_vals = [a_fn(r[...]) for r in a_refs]
    parts = [
        jnp.dot(a_vals[w_a[j]], w_refs[j][...].astype(_BF16), preferred_element_type=_F32)
        for j in range(n_w)
    ]
    if nk == 1:
        epilogue(parts, ex_refs, out_refs)
        return
    k = pl.program_id(2)

    @pl.when(k == 0)
    def _():
        for j in range(n_w):
            acc_refs[j][...] = parts[j]

    @pl.when(jnp.logical_and(k > 0, k < nk - 1))
    def _():
        for j in range(n_w):
            acc_refs[j][...] += parts[j]

    @pl.when(k == nk - 1)
    def _():
        epilogue([acc_refs[j][...] + parts[j] for j in range(n_w)], ex_refs, out_refs)


def _to_bf16(x):
    return x.astype(_BF16)


def _matmul(a_ops, w_ops, w_a, ex_ops, outs, *, grid, tm, tn, tk, epilogue, a_fn=_to_bf16):
    nk = grid[2]
    n_w = len(w_ops)
    scratch = [pltpu.VMEM((tm, tn), _F32) for _ in range(n_w)] if nk > 1 else []
    est = 0
    for arr, spec in a_ops + w_ops + ex_ops:
        blk = [d for d in spec.block_shape if d is not None]
        est += 2 * _nbytes(blk, arr.dtype)
    for sds, spec in outs:
        blk = [d for d in spec.block_shape if d is not None]
        est += 2 * _nbytes(blk, sds.dtype)
    est += n_w * (2 * tm * tn * 4 + tk * tn * 2) + 2 * tm * tn * 4
    body = functools.partial(
        _mm_body, n_a=len(a_ops), w_a=tuple(w_a), n_ex=len(ex_ops), n_out=len(outs),
        nk=nk, a_fn=a_fn, epilogue=epilogue)
    res = pl.pallas_call(
        body,
        grid=grid,
        in_specs=[s for _, s in a_ops + w_ops + ex_ops],
        out_specs=[s for _, s in outs],
        out_shape=[s for s, _ in outs],
        scratch_shapes=scratch,
        compiler_params=pltpu.CompilerParams(
            dimension_semantics=("parallel", "parallel", "arbitrary"),
            vmem_limit_bytes=_vmem_limit(est)),
    )(*[a for a, _ in a_ops + w_ops + ex_ops])
    return res


class _Tokens:
    def __init__(self, groups, seq, tm):
        self.groups, self.seq, self.tm = groups, seq, tm
        self.m = groups * seq
        if seq % tm == 0:
            self.per_row = False
        else:
            assert seq == 1 and tm == self.m
            self.per_row = True
        self.m_tiles = self.m // tm

    def mod(self, arr2d, tn, n_of=lambda n: n):
        g, d = arr2d.shape
        assert g == self.groups
        if self.per_row:
            a3 = arr2d.reshape(1, g, d)
            return a3, pl.BlockSpec((None, g, tn), lambda m, n, k: (0, 0, n_of(n)))
        a3 = arr2d.reshape(g, 1, d)
        per = self.seq // self.tm
        return a3, pl.BlockSpec((None, 1, tn), lambda m, n, k: (m // per, 0, n_of(n)))


def _row_spec(tm, tn, n_of=lambda n: n):
    return pl.BlockSpec((tm, tn), lambda m, n, k: (m, n_of(n)))


def _a_spec(tm, tk, k_of=lambda k: k):
    return pl.BlockSpec((tm, tk), lambda m, n, k: (m, k_of(k)))


def _w_spec(lead, tk, tn, k_of=lambda k: k, n_of=lambda n: n):
    nl = len(lead)
    return pl.BlockSpec((None,) * nl + (tk, tn), lambda m, n, k: tuple(lead) + (k_of(k), n_of(n)))


def _vec_spec(lead, tn, n_of=lambda n: n):
    nl = len(lead)
    return pl.BlockSpec((None,) * nl + (1, tn), lambda m, n, k: tuple(lead) + (0, n_of(n)))


def _ada(c_all, ada_w, ada_b3, l):
    m, d = c_all.shape
    n = ada_w.shape[-1]
    tn = _pick(n, 512)

    def a_fn(c):
        return (c * jax.nn.sigmoid(c)).astype(_BF16)

    def epi(accs, ex, outs):
        outs[0][...] = accs[0] + ex[0][...]

    (out,) = _matmul(
        [(c_all, pl.BlockSpec((m, d), lambda i, n_, k: (0, 0)))],
        [(ada_w, _w_spec((l,), d, tn))],
        [0],
        [(ada_b3, _vec_spec((l,), tn))],
        [(jax.ShapeDtypeStruct((m, n), _F32), pl.BlockSpec((m, tn), lambda i, n_, k: (0, n_)))],
        grid=(1, n // tn, 1), tm=m, tn=tn, tk=d, epilogue=epi, a_fn=a_fn)
    return out


def _norm_mod_kernel(x_ref, g_ref, sc_ref, sh_ref, *rest, router):
    x = x_ref[...]
    y = x * lax.rsqrt(jnp.mean(x * x, axis=-1, keepdims=True) + _EPS) * g_ref[...]
    u = y * (1.0 + sc_ref[...]) + sh_ref[...]
    if not router:
        (o_ref,) = rest
        o_ref[...] = u.astype(o_ref.dtype)
        return
    rw_ref, rb_ref, o_ref, gate_ref = rest
    o_ref[...] = u.astype(o_ref.dtype)
    n_e = rb_ref.shape[-1]
    logits = jnp.dot(u, rw_ref[...], preferred_element_type=_F32,
                     precision=lax.Precision.HIGHEST) + rb_ref[...]
    lane = lax.broadcasted_iota(jnp.int32, logits.shape, 1).astype(_F32)
    v1 = jnp.max(logits, axis=-1, keepdims=True)
    i1 = jnp.min(jnp.where(logits == v1, lane, float(n_e)), axis=-1, keepdims=True)
    rest_l = jnp.where(lane == i1, _NEG_INF, logits)
    v2 = jnp.max(rest_l, axis=-1, keepdims=True)
    i2 = jnp.min(jnp.where(rest_l == v2, lane, float(n_e)), axis=-1, keepdims=True)
    e2 = jnp.exp(v2 - v1)
    w1 = 1.0 / (1.0 + e2)
    w2 = e2 / (1.0 + e2)
    gate_ref[...] = jnp.where(lane == i1, w1, 0.0) + jnp.where(lane == i2, w2, 0.0)


def _norm_mod(x, g3, l, sc, sh, tok, router=None):
    m, d = x.shape
    tm = tok.m if tok.per_row else _pick(tok.seq, 256)
    rows = _Tokens(tok.groups, tok.seq, tm)
    sc3, sc_spec = rows.mod(sc, d)
    sh3, sh_spec = rows.mod(sh, d)
    fix = lambda spec: pl.BlockSpec(spec.block_shape, lambda i, f=spec.index_map: f(i, 0, 0))
    in_ops = [
        (x, pl.BlockSpec((tm, d), lambda i: (i, 0))),
        (g3, pl.BlockSpec((None, 1, d), lambda i: (l, 0, 0))),
        (sc3, fix(sc_spec)),
        (sh3, fix(sh_spec)),
    ]
    outs = [(jax.ShapeDtypeStruct((m, d), _BF16), pl.BlockSpec((tm, d), lambda i: (i, 0)))]
    if router is not None:
        rw, rb3, mi = router
        n_e = rw.shape[-1]
        in_ops += [
            (rw, pl.BlockSpec((None, d, n_e), lambda i: (mi, 0, 0))),
            (rb3, pl.BlockSpec((None, 1, n_e), lambda i: (mi, 0, 0))),
        ]
        outs.append((jax.ShapeDtypeStruct((m, n_e), _F32), pl.BlockSpec((tm, n_e), lambda i: (i, 0))))
    res = pl.pallas_call(
        functools.partial(_norm_mod_kernel, router=router is not None),
        grid=(m // tm,),
        in_specs=[s for _, s in in_ops],
        out_specs=[s for _, s in outs],
        out_shape=[s for s, _ in outs],
        compiler_params=pltpu.CompilerParams(
            dimension_semantics=("parallel",),
            vmem_limit_bytes=_vmem_limit(8 * tm * d * 4)),
    )(*[a for a, _ in in_ops])
    return res if router is not None else res[0]


def _proj(u, w, l, col0, n, tok, *, out_dtype, norm_g3=None, scale=1.0, bias3=None, act=None):
    m, kdim = u.shape
    tm = tok.tm
    tn = _pick(math.gcd(n, col0), 1024)
    tk = _pick(kdim, 1024)
    off = col0 // tn
    n_of = lambda j: j + off
    ex = []
    if norm_g3 is not None:
        hd = norm_g3.shape[-1]
        ex.append((norm_g3, pl.BlockSpec((None, 1, hd), lambda i, j, k: (l, 0, 0))))
    if bias3 is not None:
        ex.append((bias3, _vec_spec((l,), tn, n_of)))

    def epi(accs, exr, outs):
        acc = accs[0]
        if bias3 is not None:
            acc = acc + exr[-1][...]
        if norm_g3 is not None:
            g = exr[0][...] * scale
            hd_ = g.shape[-1]
            for c in range(tn // hd_):
                blk = acc[:, c * hd_:(c + 1) * hd_]
                y = blk * lax.rsqrt(jnp.mean(blk * blk, axis=-1, keepdims=True) + _EPS) * g
                outs[0][:, c * hd_:(c + 1) * hd_] = y.astype(out_dtype)
            return
        if act == "sigmoid":
            acc = jax.nn.sigmoid(acc)
        outs[0][...] = acc.astype(out_dtype)

    (out,) = _matmul(
        [(u, _a_spec(tm, tk))],
        [(w, _w_spec((l,), tk, tn, n_of=n_of))],
        [0], ex,
        [(jax.ShapeDtypeStruct((m, n), out_dtype), _row_spec(tm, tn))],
        grid=(tok.m_tiles, n // tn, kdim // tk), tm=tm, tn=tn, tk=tk, epilogue=epi)
    return out


def _bucket_of_distance(n):
    n = np.maximum(n, 0)
    max_exact = _N_BUCKETS // 2
    nf = np.maximum(n, 1).astype(np.float32)
    large = max_exact + (np.log(nf / np.float32(max_exact)) / np.float32(math.log(_MAX_DISTANCE / max_exact))
                         * np.float32(_N_BUCKETS - max_exact)).astype(np.int32)
    large = np.minimum(large, _N_BUCKETS - 1)
    return np.where(n < max_exact, n, large).astype(np.int32)


def _bias_kernel(rel_ref, bkt_ref, o_ref):
    h = pl.program_id(0)
    b = bkt_ref[...]
    tile = jnp.zeros(b.shape, _F32)
    for j in range(_N_BUCKETS):
        tile = jnp.where(b == j, rel_ref[j, h], tile)
    o_ref[...] = jnp.where(b < 0, _NEG_INF, tile)


def _bias_tiles(rel_bias, buckets):
    n_h = rel_bias.shape[1]
    shp = buckets.shape
    zeros = (0,) * len(shp)
    return pl.pallas_call(
        _bias_kernel,
        grid=(n_h,),
        in_specs=[pl.BlockSpec(memory_space=pltpu.SMEM),
                  pl.BlockSpec(shp, lambda h: zeros)],
        out_specs=pl.BlockSpec((None,) + shp, lambda h: (h,) + zeros),
        out_shape=jax.ShapeDtypeStruct((n_h,) + shp, _F32),
    )(rel_bias, jnp.asarray(buckets))


def _flash_buckets(t):
    r = np.arange(t)[:, None]
    c = np.arange(t)[None, :]
    far = np.full((t, t), _bucket_of_distance(np.array(_MAX_DISTANCE)), np.int32)
    prev = _bucket_of_distance(t + r - c)
    diag = np.where(c <= r, _bucket_of_distance(r - c), -1)
    return np.stack([far, prev, diag]).astype(np.int32)


def _decode_buckets(page):
    r = np.arange(page)
    far = np.full((page,), _bucket_of_distance(np.array(_MAX_DISTANCE)), np.int32)
    last = _bucket_of_distance(page - r)
    new = np.where(r == 0, _bucket_of_distance(np.array(0)), -1)
    return np.stack([far, last, new]).astype(np.int32)


def _lam_value(lam_ref, lam_init):
    lp = lam_ref[...]
    s1 = jnp.sum(lp[0:1] * lp[1:2], axis=-1, keepdims=True)
    s2 = jnp.sum(lp[2:3] * lp[3:4], axis=-1, keepdims=True)
    return jnp.exp(s1) - jnp.exp(s2) + lam_init


def _flash_kernel(lam_ref, g_ref, q_ref, k_ref, v_ref, bias_ref, o_ref, m_sc, l_sc, acc_sc, *, hd, lam_init):
    qi = pl.program_id(2)
    ki = pl.program_id(3)

    @pl.when(ki == 0)
    def _():
        m_sc[...] = jnp.full(m_sc.shape, _NEG_INF, _F32)
        l_sc[...] = jnp.zeros(l_sc.shape, _F32)
        acc_sc[...] = jnp.zeros(acc_sc.shape, _F32)

    @pl.when(ki <= qi)
    def _():
        q = q_ref[...]
        k = k_ref[...].astype(_BF16)
        v = v_ref[...].astype(_BF16)
        bias = bias_ref[...]
        for mi in range(2):
            s = lax.dot_general(q[:, mi * hd:(mi + 1) * hd], k[:, mi * hd:(mi + 1) * hd],
                                (((1,), (1,)), ((), ())), preferred_element_type=_F32) + bias
            m_old = m_sc[mi]
            m_new = jnp.maximum(m_old, jnp.max(s, axis=-1, keepdims=True))
            alpha = jnp.exp(m_old - m_new)
            p = jnp.exp(s - m_new)
            l_sc[mi] = alpha * l_sc[mi] + jnp.sum(p, axis=-1, keepdims=True)
            acc_sc[mi] = alpha * acc_sc[mi] + jnp.dot(p.astype(_BF16), v, preferred_element_type=_F32)
            m_sc[mi] = m_new

    @pl.when(ki == qi)
    def _():
        lam = _lam_value(lam_ref, lam_init)
        o = acc_sc[0] / l_sc[0] - lam * (acc_sc[1] / l_sc[1])
        y = o * lax.rsqrt(jnp.mean(o * o, axis=-1, keepdims=True) + _EPS) * g_ref[...]
        o_ref[...] = (y * (1.0 - lam_init)).astype(o_ref.dtype)


def _prompt_attention(q, k, v, bias, lam_par, subln_g3, l, lam_init, n_heads, t):
    b, s, width = q.shape
    dv = width // n_heads
    hd = dv // 2
    nq = s // t
    kv_map = lambda b_, h, qi, ki: (b_, jnp.minimum(ki, qi), h)

    def bias_map(b_, h, qi, ki):
        sel = jnp.where(ki >= qi, 2, jnp.where(ki == qi - 1, 1, 0))
        return (h, sel, 0, 0)

    return pl.pallas_call(
        functools.partial(_flash_kernel, hd=hd, lam_init=lam_init),
        grid=(b, n_heads, nq, nq),
        in_specs=[
            pl.BlockSpec((None, 4, hd), lambda b_, h, qi, ki: (l, 0, 0)),
            pl.BlockSpec((None, 1, dv), lambda b_, h, qi, ki: (l, 0, 0)),
            pl.BlockSpec((None, t, dv), lambda b_, h, qi, ki: (b_, qi, h)),
            pl.BlockSpec((None, t, dv), kv_map),
            pl.BlockSpec((None, t, dv), kv_map),
            pl.BlockSpec((None, None, t, t), bias_map),
        ],
        out_specs=pl.BlockSpec((None, t, dv), lambda b_, h, qi, ki: (b_, qi, h)),
        out_shape=jax.ShapeDtypeStruct((b, s, width), _BF16),
        scratch_shapes=[pltpu.VMEM((2, t, 1), _F32), pltpu.VMEM((2, t, 1), _F32),
                        pltpu.VMEM((2, t, dv), _F32)],
        compiler_params=pltpu.CompilerParams(
            dimension_semantics=("parallel", "parallel", "parallel", "arbitrary"),
            vmem_limit_bytes=_vmem_limit(16 * t * t * 4)),
    )(lam_par, subln_g3, q, k, v, bias)


def _decode_kernel(pt_ref, lam_ref, g_ref, q_ref, kn_ref, vn_ref, kc_ref, vc_ref, bias_ref, o_ref,
                   m_sc, l_sc, acc_sc, *, hd, n_pages, lam_init):
    p = pl.program_id(1)

    @pl.when(p == 0)
    def _():
        m_sc[...] = jnp.full(m_sc.shape, _NEG_INF, _F32)
        l_sc[...] = jnp.zeros(l_sc.shape, _F32)
        acc_sc[...] = jnp.zeros(acc_sc.shape, _F32)

    def update(kpage, vpage, bias):
        q = q_ref[...]
        for mi in range(2):
            s = jnp.sum(kpage[:, :, mi * hd:(mi + 1) * hd] * q[None, :, mi * hd:(mi + 1) * hd],
                        axis=-1, keepdims=True) + bias
            m_old = m_sc[mi]
            m_new = jnp.maximum(m_old, jnp.max(s, axis=0))
            alpha = jnp.exp(m_old - m_new)
            e = jnp.exp(s - m_new[None])
            l_sc[mi] = alpha * l_sc[mi] + jnp.sum(e, axis=0)
            acc_sc[mi] = alpha * acc_sc[mi] + jnp.sum(e * vpage, axis=0)
            m_sc[mi] = m_new

    @pl.when(p < n_pages)
    def _():
        update(kc_ref[...], vc_ref[...], bias_ref[...])

    @pl.when(p == n_pages)
    def _():
        update(kn_ref[...], vn_ref[...], bias_ref[pl.ds(0, kn_ref.shape[0])])
        lam = _lam_value(lam_ref, lam_init)
        o = acc_sc[0] / l_sc[0] - lam * (acc_sc[1] / l_sc[1])
        y = o * lax.rsqrt(jnp.mean(o * o, axis=-1, keepdims=True) + _EPS) * g_ref[...]
        o_ref[...] = (y * (1.0 - lam_init)).astype(o_ref.dtype)


def _sample_attention(q, k_new, v_new, cache_k, cache_v, page_table, bias, lam_par, subln_g3, l, lam_init):
    bd, n_heads, dv = q.shape
    hd = dv // 2
    n_pages = page_table.shape[1]
    page = cache_k.shape[2]
    pad = ((0, 0), (0, _SUBLANES - 1), (0, 0), (0, 0))
    kn = jnp.pad(k_new[:, None], pad)
    vn = jnp.pad(v_new[:, None], pad)
    last = n_pages - 1

    def cache_map(b_, p, pt):
        return (l, pt[b_, jnp.minimum(p, last)], 0, 0, 0)

    def bias_map(b_, p, pt):
        return (jnp.where(p < last, 0, jnp.where(p == last, 1, 2)), 0, 0, 0)

    grid_spec = pltpu.PrefetchScalarGridSpec(
        num_scalar_prefetch=1,
        grid=(bd, n_pages + 1),
        in_specs=[
            pl.BlockSpec((None, 4, hd), lambda b_, p, pt: (l, 0, 0)),
            pl.BlockSpec((None, 1, dv), lambda b_, p, pt: (l, 0, 0)),
            pl.BlockSpec((None, n_heads, dv), lambda b_, p, pt: (b_, 0, 0)),
            pl.BlockSpec((None, _SUBLANES, n_heads, dv), lambda b_, p, pt: (b_, 0, 0, 0)),
            pl.BlockSpec((None, _SUBLANES, n_heads, dv), lambda b_, p, pt: (b_, 0, 0, 0)),
            pl.BlockSpec((None, None, page, n_heads, dv), cache_map),
            pl.BlockSpec((None, None, page, n_heads, dv), cache_map),
            pl.BlockSpec((None, page, n_heads, 1), bias_map),
        ],
        out_specs=pl.BlockSpec((None, n_heads, dv), lambda b_, p, pt: (b_, 0, 0)),
        scratch_shapes=[pltpu.VMEM((2, n_heads, 1), _F32), pltpu.VMEM((2, n_heads, 1), _F32),
                        pltpu.VMEM((2, n_heads, dv), _F32)],
    )
    return pl.pallas_call(
        functools.partial(_decode_kernel, hd=hd, n_pages=n_pages, lam_init=lam_init),
        grid_spec=grid_spec,
        out_shape=jax.ShapeDtypeStruct((bd, n_heads, dv), _BF16),
        compiler_params=pltpu.CompilerParams(
            dimension_semantics=("parallel", "arbitrary"),
            vmem_limit_bytes=_vmem_limit(12 * page * n_heads * dv * 4)),
    )(page_table, lam_par, subln_g3, q, kn, vn, cache_k, cache_v, bias)


def _gelu_tanh(x):
    return 0.5 * x * (1.0 + jnp.tanh(math.sqrt(2.0 / math.pi) * (x + 0.044715 * (x * x * x))))


def _softplus(x):
    return jnp.maximum(x, 0.0) + jnp.log1p(jnp.exp(-jnp.abs(x)))


def _lru_gates(xc, wr_ref, wi_ref, br_ref, bi_ref, lam_ref, blk):
    xcb = xc.astype(_BF16)
    r_parts, i_parts = [], []
    for j in range(xc.shape[1] // blk):
        xj = xcb[:, j * blk:(j + 1) * blk]
        r_parts.append(jnp.dot(xj, wr_ref[j].astype(_BF16), preferred_element_type=_F32))
        i_parts.append(jnp.dot(xj, wi_ref[j].astype(_BF16), preferred_element_type=_F32))
    r = jax.nn.sigmoid(jnp.concatenate(r_parts, axis=-1) + br_ref[...])
    i = jax.nn.sigmoid(jnp.concatenate(i_parts, axis=-1) + bi_ref[...])
    log_a = -_LRU_C * r * _softplus(-lam_ref[...])
    a = jnp.exp(log_a)
    b = jnp.sqrt(-jnp.tanh(log_a) * (a * a + 1.0)) * (i * xc)
    return a, b


def _lru_seq_kernel(rx_ref, rg_ref, cp_ref, h0_ref, cw_ref, cb_ref, wr_ref, wi_ref, br_ref, bi_ref, lam_ref,
                    y_ref, hl_ref, xbuf, a_sc, b_sc, h_sc, *, tt, blk):
    t = pl.program_id(2)
    nt = pl.num_programs(2)
    hist = _CONV_WIDTH - 1
    base = _SUBLANES

    @pl.when(t == 0)
    def _():
        xbuf[pl.ds(base - hist, hist), :] = cp_ref[...]
        h_sc[...] = h0_ref[...]

    @pl.when(t > 0)
    def _():
        xbuf[pl.ds(base - hist, hist), :] = xbuf[pl.ds(base + tt - hist, hist), :]

    xbuf[pl.ds(base, tt), :] = rx_ref[...]
    xc = cb_ref[...] + sum(xbuf[pl.ds(base - hist + j, tt), :] * cw_ref[pl.ds(j, 1), :]
                           for j in range(_CONV_WIDTH))
    a, b = _lru_gates(xc, wr_ref, wi_ref, br_ref, bi_ref, lam_ref, blk)
    a_sc[...] = a
    b_sc[...] = b
    row = lax.broadcasted_iota(jnp.int32, (_SUBLANES, a.shape[1]), 0)

    def group(gi, h):
        r0 = pl.multiple_of(gi * _SUBLANES, _SUBLANES)
        ag = a_sc[pl.ds(r0, _SUBLANES), :]
        bg = b_sc[pl.ds(r0, _SUBLANES), :]
        for sft in (1, 2, 4):
            a_prev = pltpu.roll(ag, sft, axis=0)
            b_prev = pltpu.roll(bg, sft, axis=0)
            keep = row >= sft
            bg = jnp.where(keep, ag * b_prev + bg, bg)
            ag = jnp.where(keep, ag * a_prev, ag)
        hg = ag * h + bg
        b_sc[pl.ds(r0, _SUBLANES), :] = hg
        return jnp.broadcast_to(hg[_SUBLANES - 1:_SUBLANES, :], hg.shape)

    h_in = jnp.broadcast_to(h_sc[...], (_SUBLANES, a.shape[1]))
    h_out = lax.fori_loop(0, tt // _SUBLANES, group, h_in)
    h_sc[...] = h_out[0:1, :]
    y_ref[...] = (b_sc[...] * _gelu_tanh(rg_ref[...])).astype(y_ref.dtype)

    @pl.when(t == nt - 1)
    def _():
        hl_ref[...] = h_out[0:1, :]


def _lru_seq(z, rx_col, rg_col, width, conv_prev, h_prev, conv_w, conv_b3, wr, wi, br3, bi3, lam3, l, tt):
    b, s, _ = z.shape
    wb = _pick(width, 512)
    blk = wr.shape[-1]
    nb = wb // blk
    hist = _CONV_WIDTH - 1
    assert rx_col % wb == 0 and rg_col % wb == 0
    rxo, rgo = rx_col // wb, rg_col // wb
    vec = lambda: pl.BlockSpec((None, 1, wb), lambda b_, w, t: (l, 0, w))
    gate_w = lambda: pl.BlockSpec((None, nb, blk, blk), lambda b_, w, t: (l, w, 0, 0))
    y, h_last = pl.pallas_call(
        functools.partial(_lru_seq_kernel, tt=tt, blk=blk),
        grid=(b, width // wb, s // tt),
        in_specs=[
            pl.BlockSpec((None, tt, wb), lambda b_, w, t: (b_, t, rxo + w)),
            pl.BlockSpec((None, tt, wb), lambda b_, w, t: (b_, t, rgo + w)),
            pl.BlockSpec((None, hist, wb), lambda b_, w, t: (b_, 0, w)),
            pl.BlockSpec((None, 1, wb), lambda b_, w, t: (b_, 0, w)),
            pl.BlockSpec((None, _CONV_WIDTH, wb), lambda b_, w, t: (l, 0, w)),
            vec(), gate_w(), gate_w(), vec(), vec(), vec(),
        ],
        out_specs=[
            pl.BlockSpec((None, tt, wb), lambda b_, w, t: (b_, t, w)),
            pl.BlockSpec((None, 1, wb), lambda b_, w, t: (b_, 0, w)),
        ],
        out_shape=[jax.ShapeDtypeStruct((b, s, width), _BF16),
                   jax.ShapeDtypeStruct((b, 1, width), _F32)],
        scratch_shapes=[pltpu.VMEM((tt + _SUBLANES, wb), _F32), pltpu.VMEM((tt, wb), _F32),
                        pltpu.VMEM((tt, wb), _F32), pltpu.VMEM((1, wb), _F32)],
        compiler_params=pltpu.CompilerParams(
            dimension_semantics=("parallel", "parallel", "arbitrary"),
            vmem_limit_bytes=_vmem_limit(24 * tt * wb * 4)),
    )(z, z, conv_prev, h_prev[:, None, :], conv_w, conv_b3, wr, wi, br3, bi3, lam3)
    return y, h_last[:, 0, :]


def _lru_step_kernel(rx_ref, rg_ref, cp_ref, h0_ref, cw_ref, cb_ref, wr_ref, wi_ref, br_ref, bi_ref, lam_ref,
                     y_ref, h_ref, *, blk):
    hist = _CONV_WIDTH - 1
    xc = cb_ref[...] + rx_ref[...] * cw_ref[pl.ds(hist, 1), :]
    for j in range(hist):
        xc = xc + cp_ref[j] * cw_ref[pl.ds(j, 1), :]
    a, b = _lru_gates(xc, wr_ref, wi_ref, br_ref, bi_ref, lam_ref, blk)
    h = a * h0_ref[...] + b
    h_ref[...] = h
    y_ref[...] = (h * _gelu_tanh(rg_ref[...])).astype(y_ref.dtype)


def _lru_step(z, rx_col, rg_col, width, conv_prev_t, h_prev, conv_w, conv_b3, wr, wi, br3, bi3, lam3, l):
    bd = z.shape[0]
    wb = _pick(width, 512)
    blk = wr.shape[-1]
    nb = wb // blk
    hist = _CONV_WIDTH - 1
    rxo, rgo = rx_col // wb, rg_col // wb
    vec = lambda: pl.BlockSpec((None, 1, wb), lambda w: (l, 0, w))
    gate_w = lambda: pl.BlockSpec((None, nb, blk, blk), lambda w: (l, w, 0, 0))
    return pl.pallas_call(
        functools.partial(_lru_step_kernel, blk=blk),
        grid=(width // wb,),
        in_specs=[
            pl.BlockSpec((bd, wb), lambda w: (0, rxo + w)),
            pl.BlockSpec((bd, wb), lambda w: (0, rgo + w)),
            pl.BlockSpec((hist, bd, wb), lambda w: (0, 0, w)),
            pl.BlockSpec((bd, wb), lambda w: (0, w)),
            pl.BlockSpec((None, _CONV_WIDTH, wb), lambda w: (l, 0, w)),
            vec(), gate_w(), gate_w(), vec(), vec(), vec(),
        ],
        out_specs=[pl.BlockSpec((bd, wb), lambda w: (0, w)), pl.BlockSpec((bd, wb), lambda w: (0, w))],
        out_shape=[jax.ShapeDtypeStruct((bd, width), _BF16), jax.ShapeDtypeStruct((bd, width), _F32)],
        compiler_params=pltpu.CompilerParams(dimension_semantics=("parallel",)),
    )(z, z, conv_prev_t, h_prev, conv_w, conv_b3, wr, wi, br3, bi3, lam3)


def _merge(o, y, w_att, w_lru, gates, l, tok):
    m, ko = o.shape
    ky = y.shape[1]
    d = w_att.shape[-1]
    tm = tok.tm
    tn = _pick(d, 1024)
    assert ko == ky
    tk = _pick(ko, 1024)
    nb = d // tn

    def epi(accs, ex, outs):
        outs[0][...] = (ex[0][...] * accs[0] + ex[1][...] * accs[1]).astype(_BF16)

    (out,) = _matmul(
        [(o, _a_spec(tm, tk)), (y, _a_spec(tm, tk))],
        [(w_att, _w_spec((l,), tk, tn)), (w_lru, _w_spec((l,), tk, tn))],
        [0, 1],
        [(gates, _row_spec(tm, tn)), (gates, _row_spec(tm, tn, lambda n: n + nb))],
        [(jax.ShapeDtypeStruct((m, d), _BF16), _row_spec(tm, tn))],
        grid=(tok.m_tiles, nb, ko // tk), tm=tm, tn=tn, tk=tk, epilogue=epi)
    return out


def _residual_mm(a, w, lead, res, gate, tok, *, k0=0, kn=None, tk_pref=1024, partial_in=None, partial_out=False):
    m, ka = a.shape
    d = w.shape[-1]
    kn = ka if kn is None else kn
    tm = tok.tm
    tn = _pick(d, 1024)
    tk = _pick(math.gcd(kn, k0), tk_pref)
    assert ka == kn
    koff = k0 // tk
    ex = []
    if partial_in is not None:
        ex.append((partial_in, _row_spec(tm, tn)))
    if not partial_out:
        ex.append((res, _row_spec(tm, tn)))
        ex.append(tok.mod(gate, tn))

    def epi(accs, exr, outs):
        acc = accs[0]
        i = 0
        if partial_in is not None:
            acc = acc + exr[i][...]
            i += 1
        if partial_out:
            outs[0][...] = acc
        else:
            outs[0][...] = exr[i][...] + exr[i + 1][...] * acc

    (out,) = _matmul(
        [(a, _a_spec(tm, tk))],
        [(w, _w_spec(lead, tk, tn, k_of=lambda k: k + koff))],
        [0], ex,
        [(jax.ShapeDtypeStruct((m, d), _F32), _row_spec(tm, tn))],
        grid=(tok.m_tiles, d // tn, kn // tk), tm=tm, tn=tn, tk=tk, epilogue=epi)
    return out


def _swiglu_up(u, w1, w3, lead, n0, n, tok, tn_pref=1024):
    m, kdim = u.shape
    tm = tok.tm
    tn = _pick(math.gcd(n, n0), tn_pref)
    tk = _pick(kdim, 1024)
    off = n0 // tn
    n_of = lambda j: j + off

    def epi(accs, ex, outs):
        g = accs[0]
        outs[0][...] = (g * jax.nn.sigmoid(g) * accs[1]).astype(_BF16)

    (out,) = _matmul(
        [(u, _a_spec(tm, tk))],
        [(w1, _w_spec(lead, tk, tn, n_of=n_of)), (w3, _w_spec(lead, tk, tn, n_of=n_of))],
        [0, 0], [],
        [(jax.ShapeDtypeStruct((m, n), _BF16), _row_spec(tm, tn))],
        grid=(tok.m_tiles, n // tn, kdim // tk), tm=tm, tn=tn, tk=tk, epilogue=epi)
    return out


def _split_cols(n, t):
    main = (n // t) * t
    return main, n - main


def _dense_ffn(u2, w1, w3, w2, li, res, gate, tok):
    n = w1.shape[-1]
    main, tail = _split_cols(n, 1024)
    if tail == 0 or main == 0:
        h = _swiglu_up(u2, w1, w3, (li,), 0, n, tok)
        return _residual_mm(h, w2, (li,), res, gate, tok)
    h_main = _swiglu_up(u2, w1, w3, (li,), 0, main, tok)
    h_tail = _swiglu_up(u2, w1, w3, (li,), main, tail, tok, tn_pref=256)
    part = _residual_mm(h_main, w2, (li,), res, gate, tok, k0=0, kn=main, partial_out=True)
    return _residual_mm(h_tail, w2, (li,), res, gate, tok, k0=main, kn=tail, tk_pref=256, partial_in=part)


def _moe_ffn(u2, gates, w1, w3, w2, mi, res, gate, tok):
    n_e = w1.shape[1]
    n = w1.shape[-1]
    part = None
    for e in range(n_e):
        h = _swiglu_up(u2, w1, w3, (mi, e), 0, n, tok)
        part = _moe_accumulate(h, w2, (mi, e), part, gates, e, res, gate, tok, e == n_e - 1)
    return part


def _moe_accumulate(h, w2, lead, part, gates, e, res, gate, tok, last):
    m, ka = h.shape
    d = w2.shape[-1]
    tm = tok.tm
    tn = _pick(d, 1024)
    tk = _pick(ka, 1024)
    ex = [(gates, pl.BlockSpec((tm, gates.shape[1]), lambda i, j, k: (i, 0)))]
    if part is not None:
        ex.append((part, _row_spec(tm, tn)))
    if last:
        ex.append((res, _row_spec(tm, tn)))
        ex.append(tok.mod(gate, tn))

    def epi(accs, exr, outs):
        acc = exr[0][:, e:e + 1] * accs[0]
        i = 1
        if part is not None:
            acc = exr[i][...] + acc
            i += 1
        if last:
            acc = exr[i][...] + exr[i + 1][...] * acc
        outs[0][...] = acc

    (out,) = _matmul(
        [(h, _a_spec(tm, tk))],
        [(w2, _w_spec(lead, tk, tn))],
        [0], ex,
        [(jax.ShapeDtypeStruct((m, d), _F32), _row_spec(tm, tn))],
        grid=(tok.m_tiles, d // tn, ka // tk), tm=tm, tn=tn, tk=tk, epilogue=epi)
    return out


def kernel(x_prompt, x_sample, c_prompt, c_sample, cache_k, cache_v, state_conv, state_h, page_table, rel_bias, ada_w, ada_b, norm1_g, w_in, q_norm_g, k_norm_g, lam_q1, lam_k1, lam_q2, lam_k2, subln_g, conv_w, conv_b, lru_wr, lru_br, lru_wi, lru_bi, lru_lambda, w_att, w_lru, gate_w, gate_b, w_o, norm2_g, ffn_w1, ffn_w3, ffn_w2, router_w, router_b, moe_w1, moe_w3, moe_w2):
    depth = w_in.shape[0]
    bp, seq, d = x_prompt.shape
    bd = x_sample.shape[0]
    n_heads = cache_k.shape[3]
    dv = cache_v.shape[4]
    hd = q_norm_g.shape[-1]
    att_w = n_heads * dv
    lru_w = conv_w.shape[-1]
    page = cache_k.shape[2]
    assert x_sample.shape[1] == 1 and cache_k.shape[4] == 2 * hd == dv

    ptok = _Tokens(bp, seq, _pick(seq, 1024))
    stok = _Tokens(bd, 1, bd)
    t_att = _pick(seq, 256)

    vec3 = lambda a: a.reshape(a.shape[:-1] + (1, a.shape[-1]))
    ada_b3, norm1_g3, norm2_g3 = vec3(ada_b), vec3(norm1_g), vec3(norm2_g)
    q_norm_g3, k_norm_g3, subln_g3 = vec3(q_norm_g), vec3(k_norm_g), vec3(subln_g)
    conv_b3, lru_br3, lru_bi3, lam3 = vec3(conv_b), vec3(lru_br), vec3(lru_bi), vec3(lru_lambda)
    gate_b3, router_b3 = vec3(gate_b), vec3(router_b)
    lam_par = jnp.stack([lam_q1, lam_k1, lam_q2, lam_k2], axis=1)

    flash_bias = _bias_tiles(rel_bias, _flash_buckets(t_att))
    dec_bias = _bias_tiles(rel_bias, _decode_buckets(page))
    dec_bias = jnp.transpose(dec_bias, (1, 2, 0))[..., None]

    c_all = jnp.concatenate([c_prompt, c_sample], axis=0)
    pad_rows = (-c_all.shape[0]) % _SUBLANES
    c_all = jnp.pad(c_all, ((0, pad_rows), (0, 0)))

    xp = x_prompt.reshape(bp * seq, d)
    xs = x_sample.reshape(bd, d)
    zeros_conv = jnp.zeros((bp, _CONV_WIDTH - 1, lru_w), x_prompt.dtype)
    zeros_h = jnp.zeros((bp, lru_w), state_h.dtype)
    sm_scale = hd ** -0.5
    rx_col, rg_col = 3 * att_w, 3 * att_w + lru_w

    outs = {k_: [] for k_ in ("kp", "vp", "cp", "hp", "ks", "vs", "cs", "hs")}
    for l in range(depth):
        lam_init = 0.8 - 0.6 * math.exp(-0.3 * l)
        mod = _ada(c_all, ada_w, ada_b3, l)
        mods_p = [mod[:bp, j * d:(j + 1) * d] for j in range(6)]
        mods_s = [mod[bp:bp + bd, j * d:(j + 1) * d] for j in range(6)]
        dense = l % 2 == 0
        li = l // 2

        def mixer(x, tok, mods):
            sh1, sc1, g1 = mods[0], mods[1], mods[2]
            u = _norm_mod(x, norm1_g3, l, sc1, sh1, tok)
            q = _proj(u, w_in, l, 0, att_w, tok, out_dtype=_BF16 if not tok.per_row else _F32,
                      norm_g3=q_norm_g3, scale=sm_scale)
            k = _proj(u, w_in, l, att_w, att_w, tok, out_dtype=_F32, norm_g3=k_norm_g3)
            v = _proj(u, w_in, l, 2 * att_w, att_w, tok, out_dtype=_F32)
            r = _proj(u, w_in, l, 3 * att_w, 2 * lru_w, tok, out_dtype=_F32)
            gates = _proj(u, gate_w, l, 0, 2 * d, tok, out_dtype=_F32, bias3=gate_b3, act="sigmoid")
            return g1, q, k, v, r, gates

        def channel(x, x1, tok, mods):
            sh2, sc2, g2 = mods[3], mods[4], mods[5]
            if dense:
                u2 = _norm_mod(x1, norm2_g3, l, sc2, sh2, tok)
                return _dense_ffn(u2, ffn_w1, ffn_w3, ffn_w2, li, x1, g2, tok)
            u2, rgates = _norm_mod(x1, norm2_g3, l, sc2, sh2, tok, router=(router_w, router_b3, li))
            return _moe_ffn(u2, rgates, moe_w1, moe_w3, moe_w2, li, x1, g2, tok)

        g1, q, k, v, r, gates = mixer(xp, ptok, mods_p)
        o = _prompt_attention(q.reshape(bp, seq, att_w), k.reshape(bp, seq, att_w), v.reshape(bp, seq, att_w),
                              flash_bias, lam_par, subln_g3, l, lam_init, n_heads, t_att)
        r3 = r.reshape(bp, seq, 2 * lru_w)
        y, h_new = _lru_seq(r3, 0, lru_w, lru_w, zeros_conv, zeros_h, conv_w, conv_b3, lru_wr, lru_wi,
                            lru_br3, lru_bi3, lam3, l, _pick(seq, 256))
        conv_new = jnp.concatenate([zeros_conv, r3[:, -(_CONV_WIDTH - 1):, :lru_w]], axis=1)[:, -(_CONV_WIDTH - 1):]
        t_mix = _merge(o.reshape(bp * seq, att_w), y.reshape(bp * seq, lru_w), w_att, w_lru, gates, l, ptok)
        x1 = _residual_mm(t_mix, w_o, (l,), xp, g1, ptok)
        xp = channel(xp, x1, ptok, mods_p)
        outs["kp"].append(k.reshape(bp, seq, n_heads, dv))
        outs["vp"].append(v.reshape(bp, seq, n_heads, dv))
        outs["cp"].append(conv_new)
        outs["hp"].append(h_new)

        g1, q, k, v, r, gates = mixer(xs, stok, mods_s)
        o = _sample_attention(q.reshape(bd, n_heads, dv), k.reshape(bd, n_heads, dv), v.reshape(bd, n_heads, dv),
                              cache_k, cache_v, page_table, dec_bias, lam_par, subln_g3, l, lam_init)
        conv_prev = state_conv[l]
        y, h_new = _lru_step(r, 0, lru_w, lru_w, jnp.swapaxes(conv_prev, 0, 1), state_h[l], conv_w, conv_b3,
                             lru_wr, lru_wi, lru_br3, lru_bi3, lam3, l)
        conv_new = jnp.concatenate([conv_prev, r[:, None, :lru_w]], axis=1)[:, -(_CONV_WIDTH - 1):]
        t_mix = _merge(o.reshape(bd, att_w), y, w_att, w_lru, gates, l, stok)
        x1 = _residual_mm(t_mix, w_o, (l,), xs, g1, stok)
        xs = channel(xs, x1, stok, mods_s)
        outs["ks"].append(k.reshape(bd, 1, n_heads, dv))
        outs["vs"].append(v.reshape(bd, 1, n_heads, dv))
        outs["cs"].append(conv_new)
        outs["hs"].append(h_new)

    st = jnp.stack
    return (xp.reshape(bp, seq, d), xs.reshape(bd, 1, d), st(outs["kp"]), st(outs["vp"]), st(outs["cp"]),
            st(outs["hp"]), st(outs["ks"]), st(outs["vs"]), st(outs["cs"]), st(outs["hs"]))
```

```python
import functools
import math

import numpy as np
import jax
import jax.numpy as jnp
from jax import lax
from jax.experimental import pallas as pl
from jax.experimental.pallas import tpu as pltpu

_F32 = jnp.float32
_BF16 = jnp.bfloat16

_EPS = 1e-6
_LRU_C = 8.0
_CONV_WIDTH = 4
_N_BUCKETS = 32
_MAX_DISTANCE = 128
_TOP_K = 2

_LANES = 128
_SUBLANES = 8
_VMEM_CAP_BYTES = 60 * 1024 * 1024
_VMEM_MIN_BYTES = 32 * 1024 * 1024

_NEG_INF = float("-inf")


def _vmem_limit(est_bytes):
    return int(min(_VMEM_CAP_BYTES, max(_VMEM_MIN_BYTES, est_bytes)))


def _nbytes(shape, dtype):
    return int(np.prod(shape)) * jnp.dtype(dtype).itemsize


def _pick(n, pref):
    t = pref
    while t >= _LANES:
        if n % t == 0:
            return t
        t //= 2
    return n


def _mm_body(*refs, n_a, w_a, n_ex, n_out, nk, a_fn, epilogue):
    n_w = len(w_a)
    a_refs = refs[:n_a]
    w_refs = refs[n_a:n_a + n_w]
    ex_refs = refs[n_a + n_w:n_a + n_w + n_ex]
    out_refs = refs[n_a + n_w + n_ex:n_a + n_w + n_ex + n_out]
    acc_refs = refs[n_a + n_w + n_ex + n_out:]
    def products():
        a_vals = [a_fn(r[...]) for r in a_refs]
        return [jnp.dot(a_vals[w_a[j]], w_refs[j][...].astype(_BF16), preferred_element_type=_F32)
                for j in range(n_w)]

    if nk == 1:
        epilogue(products(), ex_refs, out_refs)
        return
    k = pl.program_id(2)

    @pl.when(k == 0)
    def _():
        parts = products()
        for j in range(n_w):
            acc_refs[j][...] = parts[j]

    @pl.when(jnp.logical_and(k > 0, k < nk - 1))
    def _():
        parts = products()
        for j in range(n_w):
            acc_refs[j][...] += parts[j]

    @pl.when(k == nk - 1)
    def _():
        parts = products()
        epilogue([acc_refs[j][...] + parts[j] for j in range(n_w)], ex_refs, out_refs)


def _to_bf16(x):
    return x.astype(_BF16)


_TM = 1024
_FULL_K = 4096
_TN_FULL_K = 512
_TN_TILED_K = 1024
_TK = 2048


def _mm_tiles(kdim, n, n_w=1, k0=0, n0=0):
    if kdim <= _FULL_K and k0 == 0:
        tn_pref = max(2 * _LANES, min(_TN_TILED_K, _TN_FULL_K * _FULL_K // (kdim * n_w)))
        return kdim, _pick(math.gcd(n, n0), 1 << (tn_pref.bit_length() - 1))
    return _pick(math.gcd(kdim, k0), _TK), _pick(math.gcd(n, n0), _TN_TILED_K)


def _matmul(a_ops, w_ops, w_a, ex_ops, outs, *, grid, tm, tn, tk, epilogue, name, a_fn=_to_bf16):
    nk = grid[2]
    n_w = len(w_ops)
    scratch = [pltpu.VMEM((tm, tn), _F32) for _ in range(n_w)] if nk > 1 else []
    est = 0
    for arr, spec in a_ops + w_ops + ex_ops:
        blk = [d for d in spec.block_shape if d is not None]
        est += 2 * _nbytes(blk, arr.dtype)
    for sds, spec in outs:
        blk = [d for d in spec.block_shape if d is not None]
        est += 2 * _nbytes(blk, sds.dtype)
    est += n_w * (2 * tm * tn * 4 + tk * tn * 2) + 2 * tm * tn * 4
    body = functools.partial(
        _mm_body, n_a=len(a_ops), w_a=tuple(w_a), n_ex=len(ex_ops), n_out=len(outs),
        nk=nk, a_fn=a_fn, epilogue=epilogue)
    res = pl.pallas_call(
        body,
        name=name,
        grid=grid,
        in_specs=[s for _, s in a_ops + w_ops + ex_ops],
        out_specs=[s for _, s in outs],
        out_shape=[s for s, _ in outs],
        scratch_shapes=scratch,
        compiler_params=pltpu.CompilerParams(
            dimension_semantics=("parallel", "parallel", "arbitrary"),
            vmem_limit_bytes=_vmem_limit(est)),
    )(*[a for a, _ in a_ops + w_ops + ex_ops])
    return res


class _Tokens:
    def __init__(self, groups, seq, tm):
        self.groups, self.seq, self.tm = groups, seq, tm
        self.m = groups * seq
        if seq % tm == 0:
            self.per_row = False
        else:
            assert seq == 1 and tm == self.m
            self.per_row = True
        self.m_tiles = self.m // tm

    def mod(self, arr2d, tn, n_of=lambda n: n):
        g, d = arr2d.shape
        assert g == self.groups
        if self.per_row:
            a3 = arr2d.reshape(1, g, d)
            return a3, pl.BlockSpec((None, g, tn), lambda m, n, k: (0, 0, n_of(n)))
        a3 = arr2d.reshape(g, 1, d)
        per = self.seq // self.tm
        return a3, pl.BlockSpec((None, 1, tn), lambda m, n, k: (m // per, 0, n_of(n)))


def _row_spec(tm, tn, n_of=lambda n: n):
    return pl.BlockSpec((tm, tn), lambda m, n, k: (m, n_of(n)))


def _a_spec(tm, tk, k_of=lambda k: k):
    return pl.BlockSpec((tm, tk), lambda m, n, k: (m, k_of(k)))


def _w_spec(lead, tk, tn, k_of=lambda k: k, n_of=lambda n: n):
    nl = len(lead)
    return pl.BlockSpec((None,) * nl + (tk, tn), lambda m, n, k: tuple(lead) + (k_of(k), n_of(n)))


def _vec_spec(lead, tn, n_of=lambda n: n):
    nl = len(lead)
    return pl.BlockSpec((None,) * nl + (1, tn), lambda m, n, k: tuple(lead) + (0, n_of(n)))


def _ada(c_all, ada_w, ada_b3, l):
    m, d = c_all.shape
    n = ada_w.shape[-1]
    tn = _pick(n, 512)

    def a_fn(c):
        return (c * jax.nn.sigmoid(c)).astype(_BF16)

    def epi(accs, ex, outs):
        outs[0][...] = accs[0] + ex[0][...]

    (out,) = _matmul(
        [(c_all, pl.BlockSpec((m, d), lambda i, n_, k: (0, 0)))],
        [(ada_w, _w_spec((l,), d, tn))],
        [0],
        [(ada_b3, _vec_spec((l,), tn))],
        [(jax.ShapeDtypeStruct((m, n), _F32), pl.BlockSpec((m, tn), lambda i, n_, k: (0, n_)))],
        grid=(1, n // tn, 1), tm=m, tn=tn, tk=d, epilogue=epi, a_fn=a_fn, name="ada_mod")
    return out


def _norm_mod_kernel(x_ref, g_ref, sc_ref, sh_ref, *rest, router):
    x = x_ref[...]
    y = x * lax.rsqrt(jnp.mean(x * x, axis=-1, keepdims=True) + _EPS) * g_ref[...]
    u = y * (1.0 + sc_ref[...]) + sh_ref[...]
    if not router:
        (o_ref,) = rest
        o_ref[...] = u.astype(o_ref.dtype)
        return
    rw_ref, rb_ref, o_ref, gate_ref = rest
    o_ref[...] = u.astype(o_ref.dtype)
    n_e = rb_ref.shape[-1]
    logits = jnp.dot(u, rw_ref[...], preferred_element_type=_F32,
                     precision=lax.Precision.HIGHEST) + rb_ref[...]
    lane = lax.broadcasted_iota(jnp.int32, logits.shape, 1).astype(_F32)
    v1 = jnp.max(logits, axis=-1, keepdims=True)
    i1 = jnp.min(jnp.where(logits == v1, lane, float(n_e)), axis=-1, keepdims=True)
    rest_l = jnp.where(lane == i1, _NEG_INF, logits)
    v2 = jnp.max(rest_l, axis=-1, keepdims=True)
    i2 = jnp.min(jnp.where(rest_l == v2, lane, float(n_e)), axis=-1, keepdims=True)
    e2 = jnp.exp(v2 - v1)
    w1 = 1.0 / (1.0 + e2)
    w2 = e2 / (1.0 + e2)
    gate_ref[...] = jnp.where(lane == 0.0, i1, jnp.where(lane == 1.0, i2, jnp.where(
        lane == 2.0, w1, jnp.where(lane == 3.0, w2, 0.0))))


def _norm_mod(x, g3, l, sc, sh, tok, router=None):
    m, d = x.shape
    tm = tok.m if tok.per_row else _pick(tok.seq, 256)
    rows = _Tokens(tok.groups, tok.seq, tm)
    sc3, sc_spec = rows.mod(sc, d)
    sh3, sh_spec = rows.mod(sh, d)
    fix = lambda spec: pl.BlockSpec(spec.block_shape, lambda i, f=spec.index_map: f(i, 0, 0))
    in_ops = [
        (x, pl.BlockSpec((tm, d), lambda i: (i, 0))),
        (g3, pl.BlockSpec((None, 1, d), lambda i: (l, 0, 0))),
        (sc3, fix(sc_spec)),
        (sh3, fix(sh_spec)),
    ]
    u_dtype = _BF16 if router is None else _F32
    outs = [(jax.ShapeDtypeStruct((m, d), u_dtype), pl.BlockSpec((tm, d), lambda i: (i, 0)))]
    if router is not None:
        rw, rb3, mi = router
        n_e = rw.shape[-1]
        in_ops += [
            (rw, pl.BlockSpec((None, d, n_e), lambda i: (mi, 0, 0))),
            (rb3, pl.BlockSpec((None, 1, n_e), lambda i: (mi, 0, 0))),
        ]
        outs.append((jax.ShapeDtypeStruct((m, n_e), _F32), pl.BlockSpec((tm, n_e), lambda i: (i, 0))))
    res = pl.pallas_call(
        functools.partial(_norm_mod_kernel, router=router is not None),
        name="norm_mod_route" if router is not None else "norm_mod",
        grid=(m // tm,),
        in_specs=[s for _, s in in_ops],
        out_specs=[s for _, s in outs],
        out_shape=[s for s, _ in outs],
        compiler_params=pltpu.CompilerParams(
            dimension_semantics=("parallel",),
            vmem_limit_bytes=_vmem_limit(8 * tm * d * 4)),
    )(*[a for a, _ in in_ops])
    return res if router is not None else res[0]


def _proj(u, w, l, col0, n, tok, *, out_dtype, name, norm_g3=None, scale=1.0, bias3=None, act=None):
    m, kdim = u.shape
    tm = tok.tm
    tk, tn = _mm_tiles(kdim, n, n0=col0)
    off = col0 // tn
    n_of = lambda j: j + off
    ex = []
    if norm_g3 is not None:
        hd = norm_g3.shape[-1]
        ex.append((norm_g3, pl.BlockSpec((None, 1, hd), lambda i, j, k: (l, 0, 0))))
    if bias3 is not None:
        ex.append((bias3, _vec_spec((l,), tn, n_of)))

    def epi(accs, exr, outs):
        acc = accs[0]
        if bias3 is not None:
            acc = acc + exr[-1][...]
        if norm_g3 is not None:
            g = exr[0][...] * scale
            hd_ = g.shape[-1]
            for c in range(tn // hd_):
                blk = acc[:, c * hd_:(c + 1) * hd_]
                y = blk * lax.rsqrt(jnp.mean(blk * blk, axis=-1, keepdims=True) + _EPS) * g
                outs[0][:, c * hd_:(c + 1) * hd_] = y.astype(out_dtype)
            return
        if act == "sigmoid":
            acc = jax.nn.sigmoid(acc)
        outs[0][...] = acc.astype(out_dtype)

    (out,) = _matmul(
        [(u, _a_spec(tm, tk))],
        [(w, _w_spec((l,), tk, tn, n_of=n_of))],
        [0], ex,
        [(jax.ShapeDtypeStruct((m, n), out_dtype), _row_spec(tm, tn))],
        grid=(tok.m_tiles, n // tn, kdim // tk), tm=tm, tn=tn, tk=tk, epilogue=epi, name=name)
    return out


def _bucket_of_distance(n):
    n = np.maximum(n, 0)
    max_exact = _N_BUCKETS // 2
    nf = np.maximum(n, 1).astype(np.float32)
    large = max_exact + (np.log(nf / np.float32(max_exact)) / np.float32(math.log(_MAX_DISTANCE / max_exact))
                         * np.float32(_N_BUCKETS - max_exact)).astype(np.int32)
    large = np.minimum(large, _N_BUCKETS - 1)
    return np.where(n < max_exact, n, large).astype(np.int32)


def _bias_kernel(rel_ref, bkt_ref, o_ref):
    h = pl.program_id(0)
    b = bkt_ref[...]
    tile = jnp.zeros(b.shape, _F32)
    for j in range(_N_BUCKETS):
        tile = jnp.where(b == j, rel_ref[j, h], tile)
    o_ref[...] = jnp.where(b < 0, _NEG_INF, tile)


def _bias_tiles(rel_bias, buckets):
    n_h = rel_bias.shape[1]
    shp = buckets.shape
    zeros = (0,) * len(shp)
    return pl.pallas_call(
        _bias_kernel,
        name="rel_bias_tiles",
        grid=(n_h,),
        in_specs=[pl.BlockSpec(memory_space=pltpu.SMEM),
                  pl.BlockSpec(shp, lambda h: zeros)],
        out_specs=pl.BlockSpec((None,) + shp, lambda h: (h,) + zeros),
        out_shape=jax.ShapeDtypeStruct((n_h,) + shp, _F32),
    )(rel_bias, jnp.asarray(buckets))


_FAR_BUCKET = int(_bucket_of_distance(np.array(_MAX_DISTANCE)))


def _flash_buckets(t):
    assert t >= _MAX_DISTANCE
    r = np.arange(t)[:, None]
    c = np.arange(t)[None, :]
    prev = _bucket_of_distance(t + r - c)
    diag = np.where(c <= r, _bucket_of_distance(r - c), -1)
    return np.stack([prev, diag]).astype(np.int32)


def _decode_buckets(page):
    assert page >= _MAX_DISTANCE
    r = np.arange(page)
    far = np.full((page,), _FAR_BUCKET, np.int32)
    last = _bucket_of_distance(page - r)
    new = np.where(r == 0, _bucket_of_distance(np.array(0)), -1)
    return np.stack([far, last, new]).astype(np.int32)


def _lam_value(lam_ref, lam_init):
    lp = lam_ref[...]
    s1 = jnp.sum(lp[0:1] * lp[1:2], axis=-1, keepdims=True)
    s2 = jnp.sum(lp[2:3] * lp[3:4], axis=-1, keepdims=True)
    return jnp.exp(s1) - jnp.exp(s2) + lam_init


def _softmax_init(m_sc, l_sc, acc_sc):
    m_sc[...] = jnp.full(m_sc.shape, _NEG_INF, _F32)
    l_sc[...] = jnp.zeros(l_sc.shape, _F32)
    acc_sc[...] = jnp.zeros(acc_sc.shape, _F32)


def _softmax_update(q, kb, vb, bias, m_sc, l_sc, acc_sc, hd):
    for mi in range(2):
        s = lax.dot_general(q[:, mi * hd:(mi + 1) * hd], kb[:, mi * hd:(mi + 1) * hd],
                            (((1,), (1,)), ((), ())), preferred_element_type=_F32) + bias
        m_old = m_sc[mi]
        m_new = jnp.maximum(m_old, jnp.max(s, axis=-1, keepdims=True))
        alpha = jnp.exp(m_old - m_new)
        p = jnp.exp(s - m_new)
        l_sc[mi] = alpha * l_sc[mi] + jnp.sum(p, axis=-1, keepdims=True)
        acc_sc[mi] = alpha * acc_sc[mi] + jnp.dot(p.astype(_BF16), vb, preferred_element_type=_F32)
        m_sc[mi] = m_new


def _diff_finalize(lam_ref, g_ref, o_ref, l_sc, acc_sc, lam_init):
    lam = _lam_value(lam_ref, lam_init)
    o = acc_sc[0] / l_sc[0] - lam * (acc_sc[1] / l_sc[1])
    y = o * lax.rsqrt(jnp.mean(o * o, axis=-1, keepdims=True) + _EPS) * g_ref[...]
    o_ref[...] = (y * (1.0 - lam_init)).astype(o_ref.dtype)


def _flash_kernel(qi_ref, ki_ref, rel_ref, lam_ref, g_ref, q_ref, k_ref, v_ref, bias_ref, o_ref,
                  m_sc, l_sc, acc_sc, *, hd, lam_init):
    h = pl.program_id(1)
    step = pl.program_id(2)
    qi = qi_ref[step]
    ki = ki_ref[step]

    @pl.when(ki == 0)
    def _():
        _softmax_init(m_sc, l_sc, acc_sc)

    q = q_ref[...]
    kb = k_ref[...].astype(_BF16)
    vb = v_ref[...].astype(_BF16)

    @pl.when(ki < qi - 1)
    def _():
        _softmax_update(q, kb, vb, rel_ref[_FAR_BUCKET, h], m_sc, l_sc, acc_sc, hd)

    @pl.when(ki >= qi - 1)
    def _():
        _softmax_update(q, kb, vb, bias_ref[...], m_sc, l_sc, acc_sc, hd)

    @pl.when(ki == qi)
    def _():
        _diff_finalize(lam_ref, g_ref, o_ref, l_sc, acc_sc, lam_init)


def _prompt_attention(q, k, v, rel_bias, bias, lam_par, subln_g3, l, lam_init, n_heads, t):
    b, s, width = q.shape
    dv = width // n_heads
    hd = dv // 2
    nq = s // t
    pairs = [(qi, ki) for qi in range(nq) for ki in range(qi + 1)]
    qi_tab = jnp.asarray([p_[0] for p_ in pairs], jnp.int32)
    ki_tab = jnp.asarray([p_[1] for p_ in pairs], jnp.int32)
    q_map = lambda b_, h, p, qt, kt: (b_, qt[p], h)
    kv_map = lambda b_, h, p, qt, kt: (b_, kt[p], h)
    bias_map = lambda b_, h, p, qt, kt: (h, jnp.where(kt[p] == qt[p], 1, 0), 0, 0)
    par = lambda b_, h, p, qt, kt: (l, 0, 0)
    return pl.pallas_call(
        functools.partial(_flash_kernel, hd=hd, lam_init=lam_init),
        name="prompt_attention",
        grid_spec=pltpu.PrefetchScalarGridSpec(
            num_scalar_prefetch=2,
            grid=(b, n_heads, len(pairs)),
            in_specs=[
                pl.BlockSpec(memory_space=pltpu.SMEM),
                pl.BlockSpec((None, 4, hd), par),
                pl.BlockSpec((None, 1, dv), par),
                pl.BlockSpec((None, t, dv), q_map),
                pl.BlockSpec((None, t, dv), kv_map),
                pl.BlockSpec((None, t, dv), kv_map),
                pl.BlockSpec((None, None, t, t), bias_map),
            ],
            out_specs=pl.BlockSpec((None, t, dv), q_map),
            scratch_shapes=[pltpu.VMEM((2, t, 1), _F32), pltpu.VMEM((2, t, 1), _F32),
                            pltpu.VMEM((2, t, dv), _F32)]),
        out_shape=jax.ShapeDtypeStruct((b, s, width), _BF16),
        compiler_params=pltpu.CompilerParams(
            dimension_semantics=("parallel", "parallel", "arbitrary"),
            vmem_limit_bytes=_vmem_limit(24 * t * t * 4)),
    )(qi_tab, ki_tab, rel_bias, lam_par, subln_g3, q, k, v, bias)


def _block_softmax(q, kb, vb, bias, hd):
    out = []
    for mi in range(2):
        s = lax.dot_general(q[:, mi * hd:(mi + 1) * hd], kb[:, mi * hd:(mi + 1) * hd],
                            (((1,), (1,)), ((), ())), preferred_element_type=_F32) + bias
        m = jnp.max(s, axis=-1, keepdims=True)
        p = jnp.exp(s - m)
        out.append((m, jnp.sum(p, axis=-1, keepdims=True),
                    jnp.dot(p.astype(_BF16), vb, preferred_element_type=_F32)))
    return out


def _softmax_merge(blocks, m_sc, l_sc, acc_sc):
    for mi in range(2):
        m_old = m_sc[mi]
        m_new = m_old
        for blk in blocks:
            m_new = jnp.maximum(m_new, blk[mi][0])
        alpha = jnp.exp(m_old - m_new)
        l_new = alpha * l_sc[mi]
        acc_new = alpha * acc_sc[mi]
        for blk in blocks:
            w = jnp.exp(blk[mi][0] - m_new)
            l_new = l_new + w * blk[mi][1]
            acc_new = acc_new + w * blk[mi][2]
        m_sc[mi] = m_new
        l_sc[mi] = l_new
        acc_sc[mi] = acc_new


def _decode_kernel(pt_ref, lam_ref, g_ref, q_ref, kn_ref, vn_ref, *rest, hd, n_steps, group, lam_init):
    kc_refs = rest[:group]
    vc_refs = rest[group:2 * group]
    bias_refs = rest[2 * group:3 * group]
    bias_new_ref, o_ref, m_sc, l_sc, acc_sc = rest[3 * group:]
    p = pl.program_id(1)

    @pl.when(p == 0)
    def _():
        _softmax_init(m_sc, l_sc, acc_sc)

    def block(k_ref, v_ref, bias):
        r, nh, width = k_ref.shape
        kb = k_ref[...].reshape(r * nh, width).astype(_BF16)
        vb = v_ref[...].reshape(r * nh, width).astype(_BF16)
        return _block_softmax(q_ref[...], kb, vb, bias, hd)

    @pl.when(p < n_steps)
    def _():
        blocks = [block(kc_refs[g], vc_refs[g], bias_refs[g][...]) for g in range(group)]
        _softmax_merge(blocks, m_sc, l_sc, acc_sc)

    @pl.when(p == n_steps)
    def _():
        n_new = kn_ref.shape[0] * kn_ref.shape[1]
        _softmax_merge([block(kn_ref, vn_ref, bias_new_ref[:, :n_new])], m_sc, l_sc, acc_sc)
        _diff_finalize(lam_ref, g_ref, o_ref, l_sc, acc_sc, lam_init)


_DECODE_PAGES_PER_STEP = 4


def _sample_attention(q, k_new, v_new, cache_k, cache_v, page_table, bias, lam_par, subln_g3, l, lam_init):
    bd, n_heads, dv = q.shape
    hd = dv // 2
    n_pages = page_table.shape[1]
    page = cache_k.shape[2]
    group = max(g for g in range(1, _DECODE_PAGES_PER_STEP + 1) if n_pages % g == 0)
    n_steps = n_pages // group
    pad = ((0, 0), (0, _SUBLANES - 1), (0, 0), (0, 0))
    kn = jnp.pad(k_new[:, None], pad)
    vn = jnp.pad(v_new[:, None], pad)
    last = n_pages - 1

    def page_of(p, g):
        return jnp.minimum(p, n_steps - 1) * group + g

    def cache_spec(g):
        return pl.BlockSpec((None, None, page, n_heads, dv),
                            lambda b_, p, pt: (l, pt[b_, page_of(p, g)], 0, 0, 0))

    def bias_spec(g):
        return pl.BlockSpec((None, n_heads, page * n_heads),
                            lambda b_, p, pt: (jnp.where(page_of(p, g) < last, 0, 1), 0, 0))

    par = lambda b_, p, pt: (l, 0, 0)
    grid_spec = pltpu.PrefetchScalarGridSpec(
        num_scalar_prefetch=1,
        grid=(bd, n_steps + 1),
        in_specs=[
            pl.BlockSpec((None, 4, hd), par),
            pl.BlockSpec((None, 1, dv), par),
            pl.BlockSpec((None, n_heads, dv), lambda b_, p, pt: (b_, 0, 0)),
            pl.BlockSpec((None, _SUBLANES, n_heads, dv), lambda b_, p, pt: (b_, 0, 0, 0)),
            pl.BlockSpec((None, _SUBLANES, n_heads, dv), lambda b_, p, pt: (b_, 0, 0, 0)),
        ] + [cache_spec(g) for g in range(group)] * 2 + [bias_spec(g) for g in range(group)] + [
            pl.BlockSpec((None, n_heads, page * n_heads), lambda b_, p, pt: (2, 0, 0)),
        ],
        out_specs=pl.BlockSpec((None, n_heads, dv), lambda b_, p, pt: (b_, 0, 0)),
        scratch_shapes=[pltpu.VMEM((2, n_heads, 1), _F32), pltpu.VMEM((2, n_heads, 1), _F32),
                        pltpu.VMEM((2, n_heads, dv), _F32)],
    )
    return pl.pallas_call(
        functools.partial(_decode_kernel, hd=hd, n_steps=n_steps, group=group, lam_init=lam_init),
        name="sample_attention",
        grid_spec=grid_spec,
        out_shape=jax.ShapeDtypeStruct((bd, n_heads, dv), _BF16),
        compiler_params=pltpu.CompilerParams(
            dimension_semantics=("parallel", "arbitrary"),
            vmem_limit_bytes=_vmem_limit(8 * group * page * n_heads * dv * 4)),
    )(page_table, lam_par, subln_g3, q, kn, vn, *([cache_k] * group), *([cache_v] * group),
      *([bias] * (group + 1)))


def _gelu_tanh(x):
    return 0.5 * x * (1.0 + jnp.tanh(math.sqrt(2.0 / math.pi) * (x + 0.044715 * (x * x * x))))


def _softplus(x):
    return jnp.maximum(x, 0.0) + jnp.log1p(jnp.exp(-jnp.abs(x)))


def _lru_gates(xc, wr_ref, wi_ref, br_ref, bi_ref, lam_ref, blk):
    xcb = xc.astype(_BF16)
    r_parts, i_parts = [], []
    for j in range(xc.shape[1] // blk):
        xj = xcb[:, j * blk:(j + 1) * blk]
        r_parts.append(jnp.dot(xj, wr_ref[j].astype(_BF16), preferred_element_type=_F32))
        i_parts.append(jnp.dot(xj, wi_ref[j].astype(_BF16), preferred_element_type=_F32))
    r = jax.nn.sigmoid(jnp.concatenate(r_parts, axis=-1) + br_ref[...])
    i = jax.nn.sigmoid(jnp.concatenate(i_parts, axis=-1) + bi_ref[...])
    log_a = -_LRU_C * r * _softplus(-lam_ref[...])
    a = jnp.exp(log_a)
    b = jnp.sqrt(-jnp.tanh(log_a) * (a * a + 1.0)) * (i * xc)
    return a, b


def _lru_seq_kernel(rx_ref, rg_ref, cp_ref, h0_ref, cw_ref, cb_ref, wr_ref, wi_ref, br_ref, bi_ref, lam_ref,
                    y_ref, hl_ref, xbuf, a_sc, b_sc, h_sc, *, tt, blk):
    t = pl.program_id(2)
    nt = pl.num_programs(2)
    hist = _CONV_WIDTH - 1
    base = _SUBLANES

    @pl.when(t == 0)
    def _():
        xbuf[pl.ds(base - hist, hist), :] = cp_ref[...]
        h_sc[...] = h0_ref[...]

    @pl.when(t > 0)
    def _():
        xbuf[pl.ds(base - hist, hist), :] = xbuf[pl.ds(base + tt - hist, hist), :]

    xbuf[pl.ds(base, tt), :] = rx_ref[...]
    xc = cb_ref[...] + sum(xbuf[pl.ds(base - hist + j, tt), :] * cw_ref[pl.ds(j, 1), :]
                           for j in range(_CONV_WIDTH))
    a, b = _lru_gates(xc, wr_ref, wi_ref, br_ref, bi_ref, lam_ref, blk)
    a_sc[...] = a
    b_sc[...] = b
    row = lax.broadcasted_iota(jnp.int32, (_SUBLANES, a.shape[1]), 0)

    def group(gi, h):
        r0 = pl.multiple_of(gi * _SUBLANES, _SUBLANES)
        ag = a_sc[pl.ds(r0, _SUBLANES), :]
        bg = b_sc[pl.ds(r0, _SUBLANES), :]
        for sft in (1, 2, 4):
            a_prev = pltpu.roll(ag, sft, axis=0)
            b_prev = pltpu.roll(bg, sft, axis=0)
            keep = row >= sft
            bg = jnp.where(keep, ag * b_prev + bg, bg)
            ag = jnp.where(keep, ag * a_prev, ag)
        hg = ag * h + bg
        b_sc[pl.ds(r0, _SUBLANES), :] = hg
        return jnp.broadcast_to(hg[_SUBLANES - 1:_SUBLANES, :], hg.shape)

    h_in = jnp.broadcast_to(h_sc[...], (_SUBLANES, a.shape[1]))
    h_out = lax.fori_loop(0, tt // _SUBLANES, group, h_in)
    h_sc[...] = h_out[0:1, :]
    y_ref[...] = (b_sc[...] * _gelu_tanh(rg_ref[...])).astype(y_ref.dtype)

    @pl.when(t == nt - 1)
    def _():
        hl_ref[...] = h_out[0:1, :]


def _lru_seq(z, rx_col, rg_col, width, conv_prev, h_prev, conv_w, conv_b3, wr, wi, br3, bi3, lam3, l, tt):
    b, s, _ = z.shape
    wb = _pick(width, 512)
    blk = wr.shape[-1]
    nb = wb // blk
    hist = _CONV_WIDTH - 1
    assert rx_col % wb == 0 and rg_col % wb == 0
    rxo, rgo = rx_col // wb, rg_col // wb
    vec = lambda: pl.BlockSpec((None, 1, wb), lambda b_, w, t: (l, 0, w))
    gate_w = lambda: pl.BlockSpec((None, nb, blk, blk), lambda b_, w, t: (l, w, 0, 0))
    y, h_last = pl.pallas_call(
        functools.partial(_lru_seq_kernel, tt=tt, blk=blk),
        name="rglru_seq",
        grid=(b, width // wb, s // tt),
        in_specs=[
            pl.BlockSpec((None, tt, wb), lambda b_, w, t: (b_, t, rxo + w)),
            pl.BlockSpec((None, tt, wb), lambda b_, w, t: (b_, t, rgo + w)),
            pl.BlockSpec((None, hist, wb), lambda b_, w, t: (b_, 0, w)),
            pl.BlockSpec((None, 1, wb), lambda b_, w, t: (b_, 0, w)),
            pl.BlockSpec((None, _CONV_WIDTH, wb), lambda b_, w, t: (l, 0, w)),
            vec(), gate_w(), gate_w(), vec(), vec(), vec(),
        ],
        out_specs=[
            pl.BlockSpec((None, tt, wb), lambda b_, w, t: (b_, t, w)),
            pl.BlockSpec((None, 1, wb), lambda b_, w, t: (b_, 0, w)),
        ],
        out_shape=[jax.ShapeDtypeStruct((b, s, width), _BF16),
                   jax.ShapeDtypeStruct((b, 1, width), _F32)],
        scratch_shapes=[pltpu.VMEM((tt + _SUBLANES, wb), _F32), pltpu.VMEM((tt, wb), _F32),
                        pltpu.VMEM((tt, wb), _F32), pltpu.VMEM((1, wb), _F32)],
        compiler_params=pltpu.CompilerParams(
            dimension_semantics=("parallel", "parallel", "arbitrary"),
            vmem_limit_bytes=_vmem_limit(24 * tt * wb * 4)),
    )(z, z, conv_prev, h_prev[:, None, :], conv_w, conv_b3, wr, wi, br3, bi3, lam3)
    return y, h_last[:, 0, :]


def _lru_step_kernel(rx_ref, rg_ref, cp_ref, h0_ref, cw_ref, cb_ref, wr_ref, wi_ref, br_ref, bi_ref, lam_ref,
                     y_ref, h_ref, *, blk):
    hist = _CONV_WIDTH - 1
    xc = cb_ref[...] + rx_ref[...] * cw_ref[pl.ds(hist, 1), :]
    for j in range(hist):
        xc = xc + cp_ref[j] * cw_ref[pl.ds(j, 1), :]
    a, b = _lru_gates(xc, wr_ref, wi_ref, br_ref, bi_ref, lam_ref, blk)
    h = a * h0_ref[...] + b
    h_ref[...] = h
    y_ref[...] = (h * _gelu_tanh(rg_ref[...])).astype(y_ref.dtype)


def _lru_step(z, rx_col, rg_col, width, conv_prev_t, h_prev, conv_w, conv_b3, wr, wi, br3, bi3, lam3, l):
    bd = z.shape[0]
    wb = _pick(width, 512)
    blk = wr.shape[-1]
    nb = wb // blk
    hist = _CONV_WIDTH - 1
    rxo, rgo = rx_col // wb, rg_col // wb
    vec = lambda: pl.BlockSpec((None, 1, wb), lambda w: (l, 0, w))
    gate_w = lambda: pl.BlockSpec((None, nb, blk, blk), lambda w: (l, w, 0, 0))
    return pl.pallas_call(
        functools.partial(_lru_step_kernel, blk=blk),
        name="rglru_step",
        grid=(width // wb,),
        in_specs=[
            pl.BlockSpec((bd, wb), lambda w: (0, rxo + w)),
            pl.BlockSpec((bd, wb), lambda w: (0, rgo + w)),
            pl.BlockSpec((hist, bd, wb), lambda w: (0, 0, w)),
            pl.BlockSpec((bd, wb), lambda w: (0, w)),
            pl.BlockSpec((None, _CONV_WIDTH, wb), lambda w: (l, 0, w)),
            vec(), gate_w(), gate_w(), vec(), vec(), vec(),
        ],
        out_specs=[pl.BlockSpec((bd, wb), lambda w: (0, w)), pl.BlockSpec((bd, wb), lambda w: (0, w))],
        out_shape=[jax.ShapeDtypeStruct((bd, width), _BF16), jax.ShapeDtypeStruct((bd, width), _F32)],
        compiler_params=pltpu.CompilerParams(dimension_semantics=("parallel",)),
    )(z, z, conv_prev_t, h_prev, conv_w, conv_b3, wr, wi, br3, bi3, lam3)


def _merge(o, y, w_att, w_lru, gates, l, tok):
    m, ko = o.shape
    ky = y.shape[1]
    d = w_att.shape[-1]
    tm = tok.tm
    assert ko == ky
    tk, tn = _mm_tiles(ko, d, n_w=2)
    nb = d // tn

    def epi(accs, ex, outs):
        outs[0][...] = (ex[0][...] * accs[0] + ex[1][...] * accs[1]).astype(_BF16)

    (out,) = _matmul(
        [(o, _a_spec(tm, tk)), (y, _a_spec(tm, tk))],
        [(w_att, _w_spec((l,), tk, tn)), (w_lru, _w_spec((l,), tk, tn))],
        [0, 1],
        [(gates, _row_spec(tm, tn)), (gates, _row_spec(tm, tn, lambda n: n + nb))],
        [(jax.ShapeDtypeStruct((m, d), _BF16), _row_spec(tm, tn))],
        grid=(tok.m_tiles, nb, ko // tk), tm=tm, tn=tn, tk=tk, epilogue=epi, name="branch_merge")
    return out


def _residual_mm(a, w, lead, res, gate, tok, *, name, k0=0, kn=None, partial_in=None, partial_out=False):
    m, ka = a.shape
    d = w.shape[-1]
    kn = ka if kn is None else kn
    tm = tok.tm
    tk, tn = _mm_tiles(kn, d, k0=k0)
    koff = k0 // tk
    ex = []
    if partial_in is not None:
        ex.append((partial_in, _row_spec(tm, tn)))
    if not partial_out:
        ex.append((res, _row_spec(tm, tn)))
        ex.append(tok.mod(gate, tn))

    def epi(accs, exr, outs):
        acc = accs[0]
        i = 0
        if partial_in is not None:
            acc = acc + exr[i][...]
            i += 1
        if partial_out:
            outs[0][...] = acc
        else:
            outs[0][...] = exr[i][...] + exr[i + 1][...] * acc

    (out,) = _matmul(
        [(a, _a_spec(tm, tk, k_of=lambda k: k + koff))],
        [(w, _w_spec(lead, tk, tn, k_of=lambda k: k + koff))],
        [0], ex,
        [(jax.ShapeDtypeStruct((m, d), _F32), _row_spec(tm, tn))],
        grid=(tok.m_tiles, d // tn, kn // tk), tm=tm, tn=tn, tk=tk, epilogue=epi, name=name)
    return out


def _swiglu_up(u, w1, w3, lead, tok):
    m, kdim = u.shape
    n = w1.shape[-1]
    tm = tok.tm
    tk, tn = _mm_tiles(kdim, n, n_w=2)
    n_of = lambda j: j

    def epi(accs, ex, outs):
        g = accs[0]
        outs[0][...] = (g * jax.nn.sigmoid(g) * accs[1]).astype(_BF16)

    (out,) = _matmul(
        [(u, _a_spec(tm, tk))],
        [(w1, _w_spec(lead, tk, tn, n_of=n_of)), (w3, _w_spec(lead, tk, tn, n_of=n_of))],
        [0, 0], [],
        [(jax.ShapeDtypeStruct((m, n), _BF16), _row_spec(tm, tn))],
        grid=(tok.m_tiles, n // tn, kdim // tk), tm=tm, tn=tn, tk=tk, epilogue=epi, name="swiglu_up")
    return out


def _split_cols(n, t):
    main = (n // t) * t
    return main, n - main


def _dense_ffn(u2, w1, w3, w2, li, res, gate, tok):
    n = w1.shape[-1]
    h = _swiglu_up(u2, w1, w3, (li,), tok)
    main, tail = _split_cols(n, _TK)
    if tail == 0 or main == 0:
        return _residual_mm(h, w2, (li,), res, gate, tok, name="ffn_down")
    part = _residual_mm(h, w2, (li,), res, gate, tok, k0=0, kn=main, partial_out=True, name="ffn_down_main")
    return _residual_mm(h, w2, (li,), res, gate, tok, k0=main, kn=tail, partial_in=part, name="ffn_down_tail")


_MOE_TILE = 1024
_MOE_SUB = 256


def _route_rank_kernel(route_ref, rank_ref, cnt_ref, carry):
    @pl.when(pl.program_id(0) == 0)
    def _():
        carry[...] = jnp.zeros(carry.shape, _F32)

    r = route_ref[...]
    tr, n_e = r.shape
    lane = lax.broadcasted_iota(jnp.int32, (tr, n_e), 1).astype(_F32)
    oh1 = jnp.where(lane == r[:, 0:1], 1.0, 0.0)
    oh2 = jnp.where(lane == r[:, 1:2], 1.0, 0.0)
    oh = oh1 + oh2
    tri = jnp.where(lax.broadcasted_iota(jnp.int32, (tr, tr), 1) <= lax.broadcasted_iota(jnp.int32, (tr, tr), 0),
                    1.0, 0.0).astype(_BF16)
    cum = jnp.dot(tri, oh.astype(_BF16), preferred_element_type=_F32) + carry[...]
    excl = cum - oh
    rank1 = jnp.sum(oh1 * excl, axis=-1, keepdims=True)
    rank2 = jnp.sum(oh2 * excl, axis=-1, keepdims=True)
    rank_ref[...] = jnp.where(lane == 0.0, rank1, jnp.where(lane == 1.0, rank2, 0.0))
    carry[...] = cum[tr - 1:tr, :]
    cnt_ref[...] = cum[tr - 1:tr, :]


def _route_rank(route):
    m, n_e = route.shape
    tr = _MOE_SUB
    return pl.pallas_call(
        _route_rank_kernel,
        name="moe_route_rank",
        grid=(m // tr,),
        in_specs=[pl.BlockSpec((tr, n_e), lambda i: (i, 0))],
        out_specs=[pl.BlockSpec((tr, n_e), lambda i: (i, 0)), pl.BlockSpec((1, n_e), lambda i: (0, 0))],
        out_shape=[jax.ShapeDtypeStruct((m, n_e), _F32), jax.ShapeDtypeStruct((1, n_e), _F32)],
        scratch_shapes=[pltpu.VMEM((1, n_e), _F32)],
        compiler_params=pltpu.CompilerParams(dimension_semantics=("arbitrary",)),
    )(route)


class _MoePlan:
    def __init__(self, route, n_e):
        m_all = route.shape[0]
        tm, sub = _MOE_TILE, _MOE_SUB
        self.n_tiles = (_TOP_K * m_all) // tm + n_e
        self.rows = self.n_tiles * tm
        pad = (-m_all) % sub
        rank, counts = _route_rank(jnp.pad(route, ((0, pad), (0, 0)), constant_values=-1.0))
        idx = route[:, :_TOP_K].astype(jnp.int32)
        rank = rank[:m_all, :_TOP_K].astype(jnp.int32)
        counts = counts[0].astype(jnp.int32)
        padded = (counts + sub - 1) // sub * sub
        nt_e = (padded + tm - 1) // tm
        t_end = jnp.cumsum(nt_e)
        t_start = t_end - nt_e
        used = t_end[-1]
        self.pos = t_start[idx] * tm + rank
        ti = jnp.arange(self.n_tiles, dtype=jnp.int32)
        te = jnp.minimum(jnp.searchsorted(t_end, ti, side="right").astype(jnp.int32), n_e - 1)
        live = ti < used
        te_last = jnp.minimum(jnp.searchsorted(t_end, used - 1, side="right").astype(jnp.int32), n_e - 1)
        self.tile_expert = jnp.where(live, te, te_last)
        self.tile_valid = jnp.where(live, jnp.clip(padded[te] - (ti - t_start[te]) * tm, 0, tm), 0)
        flat = self.pos.reshape(-1)
        tok = jnp.repeat(jnp.arange(m_all, dtype=jnp.int32), _TOP_K)
        self.row_token = jnp.zeros((self.rows,), jnp.int32).at[flat].set(tok)
        self.row_weight = jnp.zeros((self.rows,), _F32).at[flat].set(route[:, _TOP_K:2 * _TOP_K].reshape(-1))
        per = tm // sub
        ci = jnp.arange(self.rows // sub, dtype=jnp.int32)
        ctile = ci // per
        self.chunk_valid = ((ci % per) * sub < self.tile_valid[ctile]).astype(jnp.int32)


def _dispatch_kernel(tok_ref, cv_ref, src_ref, o_ref, buf, sem, *, sub):
    c = pl.program_id(0)
    base = c * sub

    def row_copy(r):
        return pltpu.make_async_copy(src_ref.at[pl.ds(tok_ref[base + r], 1), :], buf.at[pl.ds(r, 1), :], sem)

    @pl.when(cv_ref[c] > 0)
    def _():
        def start(r, carry):
            row_copy(r).start()
            return carry

        def wait(r, carry):
            row_copy(r).wait()
            return carry

        lax.fori_loop(0, sub, start, 0)
        lax.fori_loop(0, sub, wait, 0)
        o_ref[...] = buf[...].astype(o_ref.dtype)

    @pl.when(cv_ref[c] == 0)
    def _():
        o_ref[...] = jnp.zeros(o_ref.shape, o_ref.dtype)


def _moe_dispatch(u_all, plan):
    d = u_all.shape[1]
    sub = _MOE_SUB
    return pl.pallas_call(
        functools.partial(_dispatch_kernel, sub=sub),
        name="moe_dispatch",
        grid_spec=pltpu.PrefetchScalarGridSpec(
            num_scalar_prefetch=2,
            grid=(plan.rows // sub,),
            in_specs=[pl.BlockSpec(memory_space=pl.ANY)],
            out_specs=pl.BlockSpec((sub, d), lambda c, tok, cv: (c, 0)),
            scratch_shapes=[pltpu.VMEM((sub, d), u_all.dtype), pltpu.SemaphoreType.DMA(())]),
        out_shape=jax.ShapeDtypeStruct((plan.rows, d), _BF16),
        compiler_params=pltpu.CompilerParams(
            dimension_semantics=("arbitrary",),
            vmem_limit_bytes=_vmem_limit(6 * sub * d * 4)),
    )(plan.row_token, plan.chunk_valid, u_all)


def _gmm_body(te_ref, tv_ref, a_ref, *rest, n_w, n_ex, nk, tm, sub, epilogue):
    w_refs = rest[:n_w]
    ex_refs = rest[n_w:n_w + n_ex]
    o_ref = rest[n_w + n_ex]
    acc_refs = rest[n_w + n_ex + 1:]
    k = pl.program_id(2)
    valid = tv_ref[pl.program_id(0)]

    def accumulate(rows, compact=False):
        def products():
            a = a_ref[rows, :]
            return [jnp.dot(a, w[...].astype(_BF16), preferred_element_type=_F32) for w in w_refs]

        if nk == 1:
            o_ref[rows, :] = epilogue(products(), ex_refs, rows).astype(o_ref.dtype)
            return

        if compact:
            @pl.when(k == 0)
            def _():
                for j in range(n_w):
                    acc_refs[j][rows, :] = jnp.zeros((rows.size, acc_refs[j].shape[1]), _F32)

            parts = products()
            for j in range(n_w):
                acc_refs[j][rows, :] += parts[j]

            @pl.when(k == nk - 1)
            def _():
                full = [acc_refs[j][rows, :] for j in range(n_w)]
                o_ref[rows, :] = epilogue(full, ex_refs, rows).astype(o_ref.dtype)
            return

        @pl.when(k == 0)
        def _():
            parts = products()
            for j in range(n_w):
                acc_refs[j][rows, :] = parts[j]

        @pl.when(jnp.logical_and(k > 0, k < nk - 1))
        def _():
            parts = products()
            for j in range(n_w):
                acc_refs[j][rows, :] += parts[j]

        @pl.when(k == nk - 1)
        def _():
            parts = products()
            full = [acc_refs[j][rows, :] + parts[j] for j in range(n_w)]
            o_ref[rows, :] = epilogue(full, ex_refs, rows).astype(o_ref.dtype)

    @pl.when(valid == tm)
    def _():
        accumulate(pl.ds(0, tm))

    @pl.when(valid < tm)
    def _():
        for s in range(tm // sub):
            rows = pl.ds(s * sub, sub)

            @pl.when(s * sub < valid)
            def _():
                accumulate(rows, compact=True)

            @pl.when(jnp.logical_and(s * sub >= valid, k == nk - 1))
            def _():
                o_ref[rows, :] = jnp.zeros((sub, o_ref.shape[1]), o_ref.dtype)


def _gmm(a, w_list, mi, plan, n, ex_ops, epilogue, out_dtype, name):
    kdim = a.shape[1]
    tm, sub = _MOE_TILE, _MOE_SUB
    n_w = len(w_list)
    tk, tn = _mm_tiles(kdim, n, n_w=n_w)
    nk = kdim // tk
    live_k = lambda m, k, tv: jnp.where(tv[m] > 0, k, 0)
    a_spec = pl.BlockSpec((tm, tk), lambda m, j, k, te, tv: (m, live_k(m, k, tv)))
    w_spec = pl.BlockSpec((None, None, tk, tn),
                          lambda m, j, k, te, tv: (mi, te[m], live_k(m, k, tv), live_k(m, j, tv)))
    o_spec = pl.BlockSpec((tm, tn), lambda m, j, k, te, tv: (m, j))
    est = 2 * tm * tk * 2 + n_w * (2 * tk * tn * 4 + tk * tn * 2 + 2 * tm * tn * 4) + 4 * tm * tn * 4
    return pl.pallas_call(
        functools.partial(_gmm_body, n_w=n_w, n_ex=len(ex_ops), nk=nk, tm=tm, sub=sub, epilogue=epilogue),
        name=name,
        grid_spec=pltpu.PrefetchScalarGridSpec(
            num_scalar_prefetch=2,
            grid=(plan.n_tiles, n // tn, nk),
            in_specs=[a_spec] + [w_spec] * n_w + [s for _, s in ex_ops],
            out_specs=o_spec,
            scratch_shapes=[pltpu.VMEM((tm, tn), _F32) for _ in range(n_w)] if nk > 1 else []),
        out_shape=jax.ShapeDtypeStruct((plan.rows, n), out_dtype),
        compiler_params=pltpu.CompilerParams(
            dimension_semantics=("arbitrary", "arbitrary", "arbitrary"),
            vmem_limit_bytes=_vmem_limit(est)),
    )(plan.tile_expert, plan.tile_valid, a, *w_list, *[x for x, _ in ex_ops])


def _combine_kernel(pos_ref, y_ref, x_ref, g_ref, o_ref, ybuf, sem, *, tc):
    base = pl.program_id(0) * tc

    def row_copy(r, j):
        src = y_ref.at[pl.ds(pos_ref[_TOP_K * (base + r) + j], 1), :]
        return pltpu.make_async_copy(src, ybuf.at[j, pl.ds(r, 1), :], sem)

    def start(r, carry):
        for j in range(_TOP_K):
            row_copy(r, j).start()
        return carry

    def wait(r, carry):
        for j in range(_TOP_K):
            row_copy(r, j).wait()
        return carry

    lax.fori_loop(0, tc, start, 0)
    lax.fori_loop(0, tc, wait, 0)
    y = ybuf[0]
    for j in range(1, _TOP_K):
        y = y + ybuf[j]
    o_ref[...] = x_ref[...] + g_ref[...] * y


def _moe_combine(y, pos, res, gate, tok):
    m, d = res.shape
    tc = tok.m if tok.per_row else _pick(tok.seq, 256)
    rows = _Tokens(tok.groups, tok.seq, tc)
    g3, g_spec = rows.mod(gate, d)
    return pl.pallas_call(
        functools.partial(_combine_kernel, tc=tc),
        name="moe_combine",
        grid_spec=pltpu.PrefetchScalarGridSpec(
            num_scalar_prefetch=1,
            grid=(m // tc,),
            in_specs=[pl.BlockSpec(memory_space=pl.ANY),
                      pl.BlockSpec((tc, d), lambda i, p: (i, 0)),
                      pl.BlockSpec(g_spec.block_shape, lambda i, p, f=g_spec.index_map: f(i, 0, 0))],
            out_specs=pl.BlockSpec((tc, d), lambda i, p: (i, 0)),
            scratch_shapes=[pltpu.VMEM((_TOP_K, tc, d), _F32), pltpu.SemaphoreType.DMA(())]),
        out_shape=jax.ShapeDtypeStruct((m, d), _F32),
        compiler_params=pltpu.CompilerParams(
            dimension_semantics=("arbitrary",),
            vmem_limit_bytes=_vmem_limit(10 * tc * d * 4)),
    )(pos.reshape(-1), y, res, g3)


def _moe_ffn(u_groups, route_groups, w1, w3, w2, mi, res_groups, gate_groups, toks):
    n_e = w1.shape[1]
    n = w1.shape[-1]
    d = w2.shape[-1]
    u_all = jnp.concatenate(u_groups, axis=0)
    route = jnp.concatenate(route_groups, axis=0)
    plan = _MoePlan(route, n_e)
    a_sorted = _moe_dispatch(u_all, plan)

    def up_epi(accs, ex, rows):
        g = accs[0]
        return g * jax.nn.sigmoid(g) * accs[1]

    h = _gmm(a_sorted, [w1, w3], mi, plan, n, [], up_epi, _BF16, "moe_up")

    def down_epi(accs, ex, rows):
        return accs[0] * ex[0][rows, :]

    roww = plan.row_weight.reshape(-1, 1)
    w_spec = pl.BlockSpec((_MOE_TILE, 1), lambda m, j, k, te, tv: (m, 0))
    y = _gmm(h, [w2], mi, plan, d, [(roww, w_spec)], down_epi, _F32, "moe_down")
    outs, row0 = [], 0
    for res, gate, tok in zip(res_groups, gate_groups, toks):
        pos = lax.slice_in_dim(plan.pos, row0, row0 + tok.m, axis=0)
        outs.append(_moe_combine(y, pos, res, gate, tok))
        row0 += tok.m
    return outs


def kernel(x_prompt, x_sample, c_prompt, c_sample, cache_k, cache_v, state_conv, state_h, page_table, rel_bias, ada_w, ada_b, norm1_g, w_in, q_norm_g, k_norm_g, lam_q1, lam_k1, lam_q2, lam_k2, subln_g, conv_w, conv_b, lru_wr, lru_br, lru_wi, lru_bi, lru_lambda, w_att, w_lru, gate_w, gate_b, w_o, norm2_g, ffn_w1, ffn_w3, ffn_w2, router_w, router_b, moe_w1, moe_w3, moe_w2):
    depth = w_in.shape[0]
    bp, seq, d = x_prompt.shape
    bd = x_sample.shape[0]
    n_heads = cache_k.shape[3]
    dv = cache_v.shape[4]
    hd = q_norm_g.shape[-1]
    att_w = n_heads * dv
    lru_w = conv_w.shape[-1]
    page = cache_k.shape[2]
    assert x_sample.shape[1] == 1 and cache_k.shape[4] == 2 * hd == dv

    ptok = _Tokens(bp, seq, _pick(seq, 1024))
    stok = _Tokens(bd, 1, bd)
    t_att = _pick(seq, 512)

    vec3 = lambda a: a.reshape(a.shape[:-1] + (1, a.shape[-1]))
    ada_b3, norm1_g3, norm2_g3 = vec3(ada_b), vec3(norm1_g), vec3(norm2_g)
    q_norm_g3, k_norm_g3, subln_g3 = vec3(q_norm_g), vec3(k_norm_g), vec3(subln_g)
    conv_b3, lru_br3, lru_bi3, lam3 = vec3(conv_b), vec3(lru_br), vec3(lru_bi), vec3(lru_lambda)
    gate_b3, router_b3 = vec3(gate_b), vec3(router_b)
    lam_par = jnp.stack([lam_q1, lam_k1, lam_q2, lam_k2], axis=1)

    flash_bias = _bias_tiles(rel_bias, _flash_buckets(t_att))
    dec_bias = _bias_tiles(rel_bias, _decode_buckets(page))
    same_head = jnp.eye(n_heads, dtype=bool)[None, :, None, :]
    dec_bias = jnp.where(same_head, jnp.transpose(dec_bias, (1, 0, 2))[..., None], _NEG_INF)
    dec_bias = dec_bias.reshape(dec_bias.shape[0], n_heads, page * n_heads)

    c_all = jnp.concatenate([c_prompt, c_sample], axis=0)
    pad_rows = (-c_all.shape[0]) % _SUBLANES
    c_all = jnp.pad(c_all, ((0, pad_rows), (0, 0)))

    xp = x_prompt.reshape(bp * seq, d)
    xs = x_sample.reshape(bd, d)
    zeros_conv = jnp.zeros((bp, _CONV_WIDTH - 1, lru_w), x_prompt.dtype)
    zeros_h = jnp.zeros((bp, lru_w), state_h.dtype)
    sm_scale = hd ** -0.5
    rx_col, rg_col = 3 * att_w, 3 * att_w + lru_w

    outs = {k_: [] for k_ in ("kp", "vp", "cp", "hp", "ks", "vs", "cs", "hs")}
    for l in range(depth):
        lam_init = 0.8 - 0.6 * math.exp(-0.3 * l)
        mod = _ada(c_all, ada_w, ada_b3, l)
        mods_p = [mod[:bp, j * d:(j + 1) * d] for j in range(6)]
        mods_s = [mod[bp:bp + bd, j * d:(j + 1) * d] for j in range(6)]
        dense = l % 2 == 0
        li = l // 2

        def mixer(x, tok, mods):
            sh1, sc1, g1 = mods[0], mods[1], mods[2]
            u = _norm_mod(x, norm1_g3, l, sc1, sh1, tok)
            q = _proj(u, w_in, l, 0, att_w, tok, out_dtype=_BF16, norm_g3=q_norm_g3, scale=sm_scale,
                      name="proj_q")
            k = _proj(u, w_in, l, att_w, att_w, tok, out_dtype=_F32, norm_g3=k_norm_g3, name="proj_k")
            v = _proj(u, w_in, l, 2 * att_w, att_w, tok, out_dtype=_F32, name="proj_v")
            r = _proj(u, w_in, l, 3 * att_w, 2 * lru_w, tok, out_dtype=_F32, name="proj_lru")
            gates = _proj(u, gate_w, l, 0, 2 * d, tok, out_dtype=_F32, bias3=gate_b3, act="sigmoid",
                          name="proj_gates")
            return g1, q, k, v, r, gates

        def channel(x1s, toks, modss):
            if dense:
                res = []
                for x1, tok, mods in zip(x1s, toks, modss):
                    u2 = _norm_mod(x1, norm2_g3, l, mods[4], mods[3], tok)
                    res.append(_dense_ffn(u2, ffn_w1, ffn_w3, ffn_w2, li, x1, mods[5], tok))
                return res
            us, routes = [], []
            for x1, tok, mods in zip(x1s, toks, modss):
                u2, route = _norm_mod(x1, norm2_g3, l, mods[4], mods[3], tok, router=(router_w, router_b3, li))
                us.append(u2)
                routes.append(route)
            return _moe_ffn(us, routes, moe_w1, moe_w3, moe_w2, li, x1s, [m_[5] for m_ in modss], toks)

        g1, q, k, v, r, gates = mixer(xp, ptok, mods_p)
        o = _prompt_attention(q.reshape(bp, seq, att_w), k.reshape(bp, seq, att_w), v.reshape(bp, seq, att_w),
                              rel_bias, flash_bias, lam_par, subln_g3, l, lam_init, n_heads, t_att)
        r3 = r.reshape(bp, seq, 2 * lru_w)
        y, h_new = _lru_seq(r3, 0, lru_w, lru_w, zeros_conv, zeros_h, conv_w, conv_b3, lru_wr, lru_wi,
                            lru_br3, lru_bi3, lam3, l, _pick(seq, 256))
        conv_new = jnp.concatenate([zeros_conv, r3[:, -(_CONV_WIDTH - 1):, :lru_w]], axis=1)[:, -(_CONV_WIDTH - 1):]
        t_mix = _merge(o.reshape(bp * seq, att_w), y.reshape(bp * seq, lru_w), w_att, w_lru, gates, l, ptok)
        x1_p = _residual_mm(t_mix, w_o, (l,), xp, g1, ptok, name="out_proj")
        outs["kp"].append(k.reshape(bp, seq, n_heads, dv))
        outs["vp"].append(v.reshape(bp, seq, n_heads, dv))
        outs["cp"].append(conv_new)
        outs["hp"].append(h_new)

        g1, q, k, v, r, gates = mixer(xs, stok, mods_s)
        o = _sample_attention(q.reshape(bd, n_heads, dv), k.reshape(bd, n_heads, dv), v.reshape(bd, n_heads, dv),
                              cache_k, cache_v, page_table, dec_bias, lam_par, subln_g3, l, lam_init)
        conv_prev = state_conv[l]
        y, h_new = _lru_step(r, 0, lru_w, lru_w, jnp.swapaxes(conv_prev, 0, 1), state_h[l], conv_w, conv_b3,
                             lru_wr, lru_wi, lru_br3, lru_bi3, lam3, l)
        conv_new = jnp.concatenate([conv_prev, r[:, None, :lru_w]], axis=1)[:, -(_CONV_WIDTH - 1):]
        t_mix = _merge(o.reshape(bd, att_w), y, w_att, w_lru, gates, l, stok)
        x1_s = _residual_mm(t_mix, w_o, (l,), xs, g1, stok, name="out_proj")
        outs["ks"].append(k.reshape(bd, 1, n_heads, dv))
        outs["vs"].append(v.reshape(bd, 1, n_heads, dv))
        outs["cs"].append(conv_new)
        outs["hs"].append(h_new)

        xp, xs = channel([x1_p, x1_s], [ptok, stok], [mods_p, mods_s])

    st = jnp.stack
    return (xp.reshape(bp, seq, d), xs.reshape(bd, 1, d), st(outs["kp"]), st(outs["vp"]), st(outs["cp"]),
            st(outs["hp"]), st(outs["ks"]), st(outs["vs"]), st(outs["cs"]), st(outs["hs"]))
```

```python
import functools
import math

import numpy as np
import jax
import jax.numpy as jnp
from jax import lax
from jax.experimental import pallas as pl
from jax.experimental.pallas import tpu as pltpu

_F32 = jnp.float32
_BF16 = jnp.bfloat16

_EPS = 1e-6
_LRU_C = 8.0
_CONV_WIDTH = 4
_N_BUCKETS = 32
_MAX_DISTANCE = 128
_TOP_K = 2

_LANES = 128
_SUBLANES = 8
_VMEM_CAP_BYTES = 60 * 1024 * 1024
_VMEM_MIN_BYTES = 32 * 1024 * 1024

_NEG_INF = float("-inf")


def _vmem_limit(est_bytes):
    return int(min(_VMEM_CAP_BYTES, max(_VMEM_MIN_BYTES, est_bytes)))


def _nbytes(shape, dtype):
    return int(np.prod(shape)) * jnp.dtype(dtype).itemsize


def _pick(n, pref):
    t = pref
    while t >= _LANES:
        if n % t == 0:
            return t
        t //= 2
    return n


def _mm_body(*refs, n_a, w_a, n_ex, n_out, nk, a_fn, epilogue):
    n_w = len(w_a)
    a_refs = refs[:n_a]
    w_refs = refs[n_a:n_a + n_w]
    ex_refs = refs[n_a + n_w:n_a + n_w + n_ex]
    out_refs = refs[n_a + n_w + n_ex:n_a + n_w + n_ex + n_out]
    acc_refs = refs[n_a + n_w + n_ex + n_out:]
    def products():
        a_vals = [a_fn(r[...]) for r in a_refs]
        return [jnp.dot(a_vals[w_a[j]], w_refs[j][...].astype(_BF16), preferred_element_type=_F32)
                for j in range(n_w)]

    if nk == 1:
        epilogue(products(), ex_refs, out_refs)
        return
    k = pl.program_id(2)

    @pl.when(k == 0)
    def _():
        parts = products()
        for j in range(n_w):
            acc_refs[j][...] = parts[j]

    @pl.when(jnp.logical_and(k > 0, k < nk - 1))
    def _():
        parts = products()
        for j in range(n_w):
            acc_refs[j][...] += parts[j]

    @pl.when(k == nk - 1)
    def _():
        parts = products()
        epilogue([acc_refs[j][...] + parts[j] for j in range(n_w)], ex_refs, out_refs)


def _to_bf16(x):
    return x.astype(_BF16)


_TM = 1024
_FULL_K = 4096
_TN_FULL_K = 512
_TN_TILED_K = 1024
_TK = 2048


def _mm_tiles(kdim, n, n_w=1, k0=0, n0=0):
    if kdim <= _FULL_K and k0 == 0:
        tn_pref = max(2 * _LANES, min(_TN_TILED_K, _TN_FULL_K * _FULL_K // (kdim * n_w)))
        return kdim, _pick(math.gcd(n, n0), 1 << (tn_pref.bit_length() - 1))
    return _pick(math.gcd(kdim, k0), _TK), _pick(math.gcd(n, n0), _TN_TILED_K)


def _matmul(a_ops, w_ops, w_a, ex_ops, outs, *, grid, tm, tn, tk, epilogue, name, a_fn=_to_bf16):
    nk = grid[2]
    n_w = len(w_ops)
    scratch = [pltpu.VMEM((tm, tn), _F32) for _ in range(n_w)] if nk > 1 else []
    est = 0
    for arr, spec in a_ops + w_ops + ex_ops:
        blk = [d for d in spec.block_shape if d is not None]
        est += 2 * _nbytes(blk, arr.dtype)
    for sds, spec in outs:
        blk = [d for d in spec.block_shape if d is not None]
        est += 2 * _nbytes(blk, sds.dtype)
    est += n_w * (2 * tm * tn * 4 + tk * tn * 2) + 2 * tm * tn * 4
    body = functools.partial(
        _mm_body, n_a=len(a_ops), w_a=tuple(w_a), n_ex=len(ex_ops), n_out=len(outs),
        nk=nk, a_fn=a_fn, epilogue=epilogue)
    res = pl.pallas_call(
        body,
        name=name,
        grid=grid,
        in_specs=[s for _, s in a_ops + w_ops + ex_ops],
        out_specs=[s for _, s in outs],
        out_shape=[s for s, _ in outs],
        scratch_shapes=scratch,
        compiler_params=pltpu.CompilerParams(
            dimension_semantics=("parallel", "parallel", "arbitrary"),
            vmem_limit_bytes=_vmem_limit(est)),
    )(*[a for a, _ in a_ops + w_ops + ex_ops])
    return res


class _Tokens:
    def __init__(self, groups, seq, tm):
        self.groups, self.seq, self.tm = groups, seq, tm
        self.m = groups * seq
        if seq % tm == 0:
            self.per_row = False
        else:
            assert seq == 1 and tm == self.m
            self.per_row = True
        self.m_tiles = self.m // tm

    def mod(self, arr2d, tn, n_of=lambda n: n):
        g, d = arr2d.shape
        assert g == self.groups
        if self.per_row:
            a3 = arr2d.reshape(1, g, d)
            return a3, pl.BlockSpec((None, g, tn), lambda m, n, k: (0, 0, n_of(n)))
        a3 = arr2d.reshape(g, 1, d)
        per = self.seq // self.tm
        return a3, pl.BlockSpec((None, 1, tn), lambda m, n, k: (m // per, 0, n_of(n)))


def _row_spec(tm, tn, n_of=lambda n: n):
    return pl.BlockSpec((tm, tn), lambda m, n, k: (m, n_of(n)))


def _a_spec(tm, tk, k_of=lambda k: k):
    return pl.BlockSpec((tm, tk), lambda m, n, k: (m, k_of(k)))


def _w_spec(lead, tk, tn, k_of=lambda k: k, n_of=lambda n: n):
    nl = len(lead)
    return pl.BlockSpec((None,) * nl + (tk, tn), lambda m, n, k: tuple(lead) + (k_of(k), n_of(n)))


def _vec_spec(lead, tn, n_of=lambda n: n):
    nl = len(lead)
    return pl.BlockSpec((None,) * nl + (1, tn), lambda m, n, k: tuple(lead) + (0, n_of(n)))


def _ada(c_all, ada_w, ada_b3, l):
    m, d = c_all.shape
    n = ada_w.shape[-1]
    tn = _pick(n, 512)

    def a_fn(c):
        return (c * jax.nn.sigmoid(c)).astype(_BF16)

    def epi(accs, ex, outs):
        outs[0][...] = accs[0] + ex[0][...]

    (out,) = _matmul(
        [(c_all, pl.BlockSpec((m, d), lambda i, n_, k: (0, 0)))],
        [(ada_w, _w_spec((l,), d, tn))],
        [0],
        [(ada_b3, _vec_spec((l,), tn))],
        [(jax.ShapeDtypeStruct((m, n), _F32), pl.BlockSpec((m, tn), lambda i, n_, k: (0, n_)))],
        grid=(1, n // tn, 1), tm=m, tn=tn, tk=d, epilogue=epi, a_fn=a_fn, name="ada_mod")
    return out


def _norm_mod_kernel(x_ref, g_ref, sc_ref, sh_ref, *rest, router):
    x = x_ref[...]
    y = x * lax.rsqrt(jnp.mean(x * x, axis=-1, keepdims=True) + _EPS) * g_ref[...]
    u = y * (1.0 + sc_ref[...]) + sh_ref[...]
    if not router:
        (o_ref,) = rest
        o_ref[...] = u.astype(o_ref.dtype)
        return
    rw_ref, rb_ref, o_ref, gate_ref = rest
    o_ref[...] = u.astype(o_ref.dtype)
    n_e = rb_ref.shape[-1]
    logits = jnp.dot(u, rw_ref[...], preferred_element_type=_F32,
                     precision=lax.Precision.HIGHEST) + rb_ref[...]
    lane = lax.broadcasted_iota(jnp.int32, logits.shape, 1).astype(_F32)
    v1 = jnp.max(logits, axis=-1, keepdims=True)
    i1 = jnp.min(jnp.where(logits == v1, lane, float(n_e)), axis=-1, keepdims=True)
    rest_l = jnp.where(lane == i1, _NEG_INF, logits)
    v2 = jnp.max(rest_l, axis=-1, keepdims=True)
    i2 = jnp.min(jnp.where(rest_l == v2, lane, float(n_e)), axis=-1, keepdims=True)
    e2 = jnp.exp(v2 - v1)
    w1 = 1.0 / (1.0 + e2)
    w2 = e2 / (1.0 + e2)
    gate_ref[...] = jnp.where(lane == 0.0, i1, jnp.where(lane == 1.0, i2, jnp.where(
        lane == 2.0, w1, jnp.where(lane == 3.0, w2, 0.0))))


def _norm_mod(x, g3, l, sc, sh, tok, router=None):
    m, d = x.shape
    tm = tok.m if tok.per_row else _pick(tok.seq, 256)
    rows = _Tokens(tok.groups, tok.seq, tm)
    sc3, sc_spec = rows.mod(sc, d)
    sh3, sh_spec = rows.mod(sh, d)
    fix = lambda spec: pl.BlockSpec(spec.block_shape, lambda i, f=spec.index_map: f(i, 0, 0))
    in_ops = [
        (x, pl.BlockSpec((tm, d), lambda i: (i, 0))),
        (g3, pl.BlockSpec((None, 1, d), lambda i: (l, 0, 0))),
        (sc3, fix(sc_spec)),
        (sh3, fix(sh_spec)),
    ]
    u_dtype = _BF16 if router is None else _F32
    outs = [(jax.ShapeDtypeStruct((m, d), u_dtype), pl.BlockSpec((tm, d), lambda i: (i, 0)))]
    if router is not None:
        rw, rb3, mi = router
        n_e = rw.shape[-1]
        in_ops += [
            (rw, pl.BlockSpec((None, d, n_e), lambda i: (mi, 0, 0))),
            (rb3, pl.BlockSpec((None, 1, n_e), lambda i: (mi, 0, 0))),
        ]
        outs.append((jax.ShapeDtypeStruct((m, n_e), _F32), pl.BlockSpec((tm, n_e), lambda i: (i, 0))))
    res = pl.pallas_call(
        functools.partial(_norm_mod_kernel, router=router is not None),
        name="norm_mod_route" if router is not None else "norm_mod",
        grid=(m // tm,),
        in_specs=[s for _, s in in_ops],
        out_specs=[s for _, s in outs],
        out_shape=[s for s, _ in outs],
        compiler_params=pltpu.CompilerParams(
            dimension_semantics=("parallel",),
            vmem_limit_bytes=_vmem_limit(8 * tm * d * 4)),
    )(*[a for a, _ in in_ops])
    return res if router is not None else res[0]


def _proj(u, w, l, col0, n, tok, *, out_dtype, name, norm_g3=None, scale=1.0, bias3=None, act=None):
    m, kdim = u.shape
    tm = tok.tm
    tk, tn = _mm_tiles(kdim, n, n0=col0)
    off = col0 // tn
    n_of = lambda j: j + off
    ex = []
    if norm_g3 is not None:
        hd = norm_g3.shape[-1]
        ex.append((norm_g3, pl.BlockSpec((None, 1, hd), lambda i, j, k: (l, 0, 0))))
    if bias3 is not None:
        ex.append((bias3, _vec_spec((l,), tn, n_of)))

    def epi(accs, exr, outs):
        acc = accs[0]
        if bias3 is not None:
            acc = acc + exr[-1][...]
        if norm_g3 is not None:
            g = exr[0][...] * scale
            hd_ = g.shape[-1]
            for c in range(tn // hd_):
                blk = acc[:, c * hd_:(c + 1) * hd_]
                y = blk * lax.rsqrt(jnp.mean(blk * blk, axis=-1, keepdims=True) + _EPS) * g
                outs[0][:, c * hd_:(c + 1) * hd_] = y.astype(out_dtype)
            return
        if act == "sigmoid":
            acc = jax.nn.sigmoid(acc)
        outs[0][...] = acc.astype(out_dtype)

    (out,) = _matmul(
        [(u, _a_spec(tm, tk))],
        [(w, _w_spec((l,), tk, tn, n_of=n_of))],
        [0], ex,
        [(jax.ShapeDtypeStruct((m, n), out_dtype), _row_spec(tm, tn))],
        grid=(tok.m_tiles, n // tn, kdim // tk), tm=tm, tn=tn, tk=tk, epilogue=epi, name=name)
    return out


def _bucket_of_distance(n):
    n = np.maximum(n, 0)
    max_exact = _N_BUCKETS // 2
    nf = np.maximum(n, 1).astype(np.float32)
    large = max_exact + (np.log(nf / np.float32(max_exact)) / np.float32(math.log(_MAX_DISTANCE / max_exact))
                         * np.float32(_N_BUCKETS - max_exact)).astype(np.int32)
    large = np.minimum(large, _N_BUCKETS - 1)
    return np.where(n < max_exact, n, large).astype(np.int32)


def _bias_kernel(rel_ref, bkt_ref, o_ref):
    h = pl.program_id(0)
    b = bkt_ref[...]
    tile = jnp.zeros(b.shape, _F32)
    for j in range(_N_BUCKETS):
        tile = jnp.where(b == j, rel_ref[j, h], tile)
    o_ref[...] = jnp.where(b < 0, _NEG_INF, tile)


def _bias_tiles(rel_bias, buckets):
    n_h = rel_bias.shape[1]
    shp = buckets.shape
    zeros = (0,) * len(shp)
    return pl.pallas_call(
        _bias_kernel,
        name="rel_bias_tiles",
        grid=(n_h,),
        in_specs=[pl.BlockSpec(memory_space=pltpu.SMEM),
                  pl.BlockSpec(shp, lambda h: zeros)],
        out_specs=pl.BlockSpec((None,) + shp, lambda h: (h,) + zeros),
        out_shape=jax.ShapeDtypeStruct((n_h,) + shp, _F32),
    )(rel_bias, jnp.asarray(buckets))


_FAR_BUCKET = int(_bucket_of_distance(np.array(_MAX_DISTANCE)))


def _flash_buckets(t):
    assert t >= _MAX_DISTANCE
    r = np.arange(t)[:, None]
    c = np.arange(t)[None, :]
    prev = _bucket_of_distance(t + r - c)
    diag = np.where(c <= r, _bucket_of_distance(r - c), -1)
    return np.stack([prev, diag]).astype(np.int32)


def _decode_buckets(page):
    assert page >= _MAX_DISTANCE
    r = np.arange(page)
    far = np.full((page,), _FAR_BUCKET, np.int32)
    last = _bucket_of_distance(page - r)
    new = np.where(r == 0, _bucket_of_distance(np.array(0)), -1)
    return np.stack([far, last, new]).astype(np.int32)


def _lam_value(lam_ref, lam_init):
    lp = lam_ref[...]
    s1 = jnp.sum(lp[0:1] * lp[1:2], axis=-1, keepdims=True)
    s2 = jnp.sum(lp[2:3] * lp[3:4], axis=-1, keepdims=True)
    return jnp.exp(s1) - jnp.exp(s2) + lam_init


def _softmax_init(m_sc, l_sc, acc_sc):
    m_sc[...] = jnp.full(m_sc.shape, _NEG_INF, _F32)
    l_sc[...] = jnp.zeros(l_sc.shape, _F32)
    acc_sc[...] = jnp.zeros(acc_sc.shape, _F32)


def _softmax_update(q, kb, vb, bias, m_sc, l_sc, acc_sc, hd):
    for mi in range(2):
        s = lax.dot_general(q[:, mi * hd:(mi + 1) * hd], kb[:, mi * hd:(mi + 1) * hd],
                            (((1,), (1,)), ((), ())), preferred_element_type=_F32) + bias
        m_old = m_sc[mi]
        m_new = jnp.maximum(m_old, jnp.max(s, axis=-1, keepdims=True))
        alpha = jnp.exp(m_old - m_new)
        p = jnp.exp(s - m_new)
        l_sc[mi] = alpha * l_sc[mi] + jnp.sum(p, axis=-1, keepdims=True)
        acc_sc[mi] = alpha * acc_sc[mi] + jnp.dot(p.astype(_BF16), vb, preferred_element_type=_F32)
        m_sc[mi] = m_new


def _diff_finalize(lam_ref, g_ref, o_ref, l_sc, acc_sc, lam_init):
    lam = _lam_value(lam_ref, lam_init)
    o = acc_sc[0] / l_sc[0] - lam * (acc_sc[1] / l_sc[1])
    y = o * lax.rsqrt(jnp.mean(o * o, axis=-1, keepdims=True) + _EPS) * g_ref[...]
    o_ref[...] = (y * (1.0 - lam_init)).astype(o_ref.dtype)


def _flash_kernel(qi_ref, ki_ref, rel_ref, lam_ref, g_ref, q_ref, k_ref, v_ref, bias_ref, o_ref,
                  m_sc, l_sc, acc_sc, *, hd, lam_init):
    h = pl.program_id(1)
    step = pl.program_id(2)
    qi = qi_ref[step]
    ki = ki_ref[step]

    @pl.when(ki == 0)
    def _():
        _softmax_init(m_sc, l_sc, acc_sc)

    q = q_ref[...]
    kb = k_ref[...].astype(_BF16)
    vb = v_ref[...].astype(_BF16)

    @pl.when(ki < qi - 1)
    def _():
        _softmax_update(q, kb, vb, rel_ref[_FAR_BUCKET, h], m_sc, l_sc, acc_sc, hd)

    @pl.when(ki >= qi - 1)
    def _():
        _softmax_update(q, kb, vb, bias_ref[...], m_sc, l_sc, acc_sc, hd)

    @pl.when(ki == qi)
    def _():
        _diff_finalize(lam_ref, g_ref, o_ref, l_sc, acc_sc, lam_init)


def _prompt_attention(q, k, v, rel_bias, bias, lam_par, subln_g3, l, lam_init, n_heads, t):
    b, s, width = q.shape
    dv = width // n_heads
    hd = dv // 2
    nq = s // t
    pairs = [(qi, ki) for qi in range(nq) for ki in range(qi + 1)]
    qi_tab = jnp.asarray([p_[0] for p_ in pairs], jnp.int32)
    ki_tab = jnp.asarray([p_[1] for p_ in pairs], jnp.int32)
    q_map = lambda b_, h, p, qt, kt: (b_, qt[p], h)
    kv_map = lambda b_, h, p, qt, kt: (b_, kt[p], h)
    bias_map = lambda b_, h, p, qt, kt: (h, jnp.where(kt[p] == qt[p], 1, 0), 0, 0)
    par = lambda b_, h, p, qt, kt: (l, 0, 0)
    return pl.pallas_call(
        functools.partial(_flash_kernel, hd=hd, lam_init=lam_init),
        name="prompt_attention",
        grid_spec=pltpu.PrefetchScalarGridSpec(
            num_scalar_prefetch=2,
            grid=(b, n_heads, len(pairs)),
            in_specs=[
                pl.BlockSpec(memory_space=pltpu.SMEM),
                pl.BlockSpec((None, 4, hd), par),
                pl.BlockSpec((None, 1, dv), par),
                pl.BlockSpec((None, t, dv), q_map),
                pl.BlockSpec((None, t, dv), kv_map),
                pl.BlockSpec((None, t, dv), kv_map),
                pl.BlockSpec((None, None, t, t), bias_map),
            ],
            out_specs=pl.BlockSpec((None, t, dv), q_map),
            scratch_shapes=[pltpu.VMEM((2, t, 1), _F32), pltpu.VMEM((2, t, 1), _F32),
                            pltpu.VMEM((2, t, dv), _F32)]),
        out_shape=jax.ShapeDtypeStruct((b, s, width), _BF16),
        compiler_params=pltpu.CompilerParams(
            dimension_semantics=("parallel", "parallel", "arbitrary"),
            vmem_limit_bytes=_vmem_limit(24 * t * t * 4)),
    )(qi_tab, ki_tab, rel_bias, lam_par, subln_g3, q, k, v, bias)


def _block_softmax(q, kb, vb, bias, hd):
    out = []
    for mi in range(2):
        s = lax.dot_general(q[:, mi * hd:(mi + 1) * hd], kb[:, mi * hd:(mi + 1) * hd],
                            (((1,), (1,)), ((), ())), preferred_element_type=_F32) + bias
        m = jnp.max(s, axis=-1, keepdims=True)
        p = jnp.exp(s - m)
        out.append((m, jnp.sum(p, axis=-1, keepdims=True),
                    jnp.dot(p.astype(_BF16), vb, preferred_element_type=_F32)))
    return out


def _softmax_merge(blocks, m_sc, l_sc, acc_sc):
    for mi in range(2):
        m_old = m_sc[mi]
        m_new = m_old
        for blk in blocks:
            m_new = jnp.maximum(m_new, blk[mi][0])
        alpha = jnp.exp(m_old - m_new)
        l_new = alpha * l_sc[mi]
        acc_new = alpha * acc_sc[mi]
        for blk in blocks:
            w = jnp.exp(blk[mi][0] - m_new)
            l_new = l_new + w * blk[mi][1]
            acc_new = acc_new + w * blk[mi][2]
        m_sc[mi] = m_new
        l_sc[mi] = l_new
        acc_sc[mi] = acc_new


def _decode_kernel(pt_ref, lam_ref, g_ref, q_ref, kn_ref, vn_ref, *rest, hd, n_steps, group, lam_init):
    kc_refs = rest[:group]
    vc_refs = rest[group:2 * group]
    bias_refs = rest[2 * group:3 * group]
    bias_new_ref, o_ref, m_sc, l_sc, acc_sc = rest[3 * group:]
    p = pl.program_id(1)

    @pl.when(p == 0)
    def _():
        _softmax_init(m_sc, l_sc, acc_sc)

    def block(k_ref, v_ref, bias):
        r, nh, width = k_ref.shape
        kb = k_ref[...].reshape(r * nh, width).astype(_BF16)
        vb = v_ref[...].reshape(r * nh, width).astype(_BF16)
        return _block_softmax(q_ref[...], kb, vb, bias, hd)

    @pl.when(p < n_steps)
    def _():
        blocks = [block(kc_refs[g], vc_refs[g], bias_refs[g][...]) for g in range(group)]
        _softmax_merge(blocks, m_sc, l_sc, acc_sc)

    @pl.when(p == n_steps)
    def _():
        n_new = kn_ref.shape[0] * kn_ref.shape[1]
        _softmax_merge([block(kn_ref, vn_ref, bias_new_ref[:, :n_new])], m_sc, l_sc, acc_sc)
        _diff_finalize(lam_ref, g_ref, o_ref, l_sc, acc_sc, lam_init)


_DECODE_PAGES_PER_STEP = 4


def _sample_attention(q, k_new, v_new, cache_k, cache_v, page_table, bias, lam_par, subln_g3, l, lam_init):
    bd, n_heads, dv = q.shape
    hd = dv // 2
    n_pages = page_table.shape[1]
    page = cache_k.shape[2]
    group = max(g for g in range(1, _DECODE_PAGES_PER_STEP + 1) if n_pages % g == 0)
    n_steps = n_pages // group
    pad = ((0, 0), (0, _SUBLANES - 1), (0, 0), (0, 0))
    kn = jnp.pad(k_new[:, None], pad)
    vn = jnp.pad(v_new[:, None], pad)
    last = n_pages - 1

    def page_of(p, g):
        return jnp.minimum(p, n_steps - 1) * group + g

    def cache_spec(g):
        return pl.BlockSpec((None, None, page, n_heads, dv),
                            lambda b_, p, pt: (l, pt[b_, page_of(p, g)], 0, 0, 0))

    def bias_spec(g):
        return pl.BlockSpec((None, n_heads, page * n_heads),
                            lambda b_, p, pt: (jnp.where(page_of(p, g) < last, 0, 1), 0, 0))

    par = lambda b_, p, pt: (l, 0, 0)
    grid_spec = pltpu.PrefetchScalarGridSpec(
        num_scalar_prefetch=1,
        grid=(bd, n_steps + 1),
        in_specs=[
            pl.BlockSpec((None, 4, hd), par),
            pl.BlockSpec((None, 1, dv), par),
            pl.BlockSpec((None, n_heads, dv), lambda b_, p, pt: (b_, 0, 0)),
            pl.BlockSpec((None, _SUBLANES, n_heads, dv), lambda b_, p, pt: (b_, 0, 0, 0)),
            pl.BlockSpec((None, _SUBLANES, n_heads, dv), lambda b_, p, pt: (b_, 0, 0, 0)),
        ] + [cache_spec(g) for g in range(group)] * 2 + [bias_spec(g) for g in range(group)] + [
            pl.BlockSpec((None, n_heads, page * n_heads), lambda b_, p, pt: (2, 0, 0)),
        ],
        out_specs=pl.BlockSpec((None, n_heads, dv), lambda b_, p, pt: (b_, 0, 0)),
        scratch_shapes=[pltpu.VMEM((2, n_heads, 1), _F32), pltpu.VMEM((2, n_heads, 1), _F32),
                        pltpu.VMEM((2, n_heads, dv), _F32)],
    )
    return pl.pallas_call(
        functools.partial(_decode_kernel, hd=hd, n_steps=n_steps, group=group, lam_init=lam_init),
        name="sample_attention",
        grid_spec=grid_spec,
        out_shape=jax.ShapeDtypeStruct((bd, n_heads, dv), _BF16),
        compiler_params=pltpu.CompilerParams(
            dimension_semantics=("parallel", "arbitrary"),
            vmem_limit_bytes=_vmem_limit(8 * group * page * n_heads * dv * 4)),
    )(page_table, lam_par, subln_g3, q, kn, vn, *([cache_k] * group), *([cache_v] * group),
      *([bias] * (group + 1)))


def _gelu_tanh(x):
    return 0.5 * x * (1.0 + jnp.tanh(math.sqrt(2.0 / math.pi) * (x + 0.044715 * (x * x * x))))


def _softplus(x):
    return jnp.maximum(x, 0.0) + jnp.log1p(jnp.exp(-jnp.abs(x)))


def _lru_gates(xc, wr_ref, wi_ref, br_ref, bi_ref, lam_ref, blk):
    xcb = xc.astype(_BF16)
    r_parts, i_parts = [], []
    for j in range(xc.shape[1] // blk):
        xj = xcb[:, j * blk:(j + 1) * blk]
        r_parts.append(jnp.dot(xj, wr_ref[j].astype(_BF16), preferred_element_type=_F32))
        i_parts.append(jnp.dot(xj, wi_ref[j].astype(_BF16), preferred_element_type=_F32))
    r = jax.nn.sigmoid(jnp.concatenate(r_parts, axis=-1) + br_ref[...])
    i = jax.nn.sigmoid(jnp.concatenate(i_parts, axis=-1) + bi_ref[...])
    log_a = -_LRU_C * r * _softplus(-lam_ref[...])
    a = jnp.exp(log_a)
    b = jnp.sqrt(-jnp.tanh(log_a) * (a * a + 1.0)) * (i * xc)
    return a, b


def _lru_seq_kernel(rx_ref, rg_ref, cp_ref, h0_ref, cw_ref, cb_ref, wr_ref, wi_ref, br_ref, bi_ref, lam_ref,
                    y_ref, hl_ref, xbuf, a_sc, b_sc, h_sc, *, tt, blk):
    t = pl.program_id(2)
    nt = pl.num_programs(2)
    hist = _CONV_WIDTH - 1
    base = _SUBLANES

    @pl.when(t == 0)
    def _():
        xbuf[pl.ds(base - hist, hist), :] = cp_ref[...]
        h_sc[...] = h0_ref[...]

    @pl.when(t > 0)
    def _():
        xbuf[pl.ds(base - hist, hist), :] = xbuf[pl.ds(base + tt - hist, hist), :]

    xbuf[pl.ds(base, tt), :] = rx_ref[...]
    xc = cb_ref[...] + sum(xbuf[pl.ds(base - hist + j, tt), :] * cw_ref[pl.ds(j, 1), :]
                           for j in range(_CONV_WIDTH))
    a, b = _lru_gates(xc, wr_ref, wi_ref, br_ref, bi_ref, lam_ref, blk)
    a_sc[...] = a
    b_sc[...] = b
    row = lax.broadcasted_iota(jnp.int32, (_SUBLANES, a.shape[1]), 0)

    def group(gi, h):
        r0 = pl.multiple_of(gi * _SUBLANES, _SUBLANES)
        ag = a_sc[pl.ds(r0, _SUBLANES), :]
        bg = b_sc[pl.ds(r0, _SUBLANES), :]
        for sft in (1, 2, 4):
            a_prev = pltpu.roll(ag, sft, axis=0)
            b_prev = pltpu.roll(bg, sft, axis=0)
            keep = row >= sft
            bg = jnp.where(keep, ag * b_prev + bg, bg)
            ag = jnp.where(keep, ag * a_prev, ag)
        hg = ag * h + bg
        b_sc[pl.ds(r0, _SUBLANES), :] = hg
        return jnp.broadcast_to(hg[_SUBLANES - 1:_SUBLANES, :], hg.shape)

    h_in = jnp.broadcast_to(h_sc[...], (_SUBLANES, a.shape[1]))
    h_out = lax.fori_loop(0, tt // _SUBLANES, group, h_in)
    h_sc[...] = h_out[0:1, :]
    y_ref[...] = (b_sc[...] * _gelu_tanh(rg_ref[...])).astype(y_ref.dtype)

    @pl.when(t == nt - 1)
    def _():
        hl_ref[...] = h_out[0:1, :]


def _lru_seq(z, rx_col, rg_col, width, conv_prev, h_prev, conv_w, conv_b3, wr, wi, br3, bi3, lam3, l, tt):
    b, s, _ = z.shape
    wb = _pick(width, 512)
    blk = wr.shape[-1]
    nb = wb // blk
    hist = _CONV_WIDTH - 1
    assert rx_col % wb == 0 and rg_col % wb == 0
    rxo, rgo = rx_col // wb, rg_col // wb
    vec = lambda: pl.BlockSpec((None, 1, wb), lambda b_, w, t: (l, 0, w))
    gate_w = lambda: pl.BlockSpec((None, nb, blk, blk), lambda b_, w, t: (l, w, 0, 0))
    y, h_last = pl.pallas_call(
        functools.partial(_lru_seq_kernel, tt=tt, blk=blk),
        name="rglru_seq",
        grid=(b, width // wb, s // tt),
        in_specs=[
            pl.BlockSpec((None, tt, wb), lambda b_, w, t: (b_, t, rxo + w)),
            pl.BlockSpec((None, tt, wb), lambda b_, w, t: (b_, t, rgo + w)),
            pl.BlockSpec((None, hist, wb), lambda b_, w, t: (b_, 0, w)),
            pl.BlockSpec((None, 1, wb), lambda b_, w, t: (b_, 0, w)),
            pl.BlockSpec((None, _CONV_WIDTH, wb), lambda b_, w, t: (l, 0, w)),
            vec(), gate_w(), gate_w(), vec(), vec(), vec(),
        ],
        out_specs=[
            pl.BlockSpec((None, tt, wb), lambda b_, w, t: (b_, t, w)),
            pl.BlockSpec((None, 1, wb), lambda b_, w, t: (b_, 0, w)),
        ],
        out_shape=[jax.ShapeDtypeStruct((b, s, width), _BF16),
                   jax.ShapeDtypeStruct((b, 1, width), _F32)],
        scratch_shapes=[pltpu.VMEM((tt + _SUBLANES, wb), _F32), pltpu.VMEM((tt, wb), _F32),
                        pltpu.VMEM((tt, wb), _F32), pltpu.VMEM((1, wb), _F32)],
        compiler_params=pltpu.CompilerParams(
            dimension_semantics=("parallel", "parallel", "arbitrary"),
            vmem_limit_bytes=_vmem_limit(24 * tt * wb * 4)),
    )(z, z, conv_prev, h_prev[:, None, :], conv_w, conv_b3, wr, wi, br3, bi3, lam3)
    return y, h_last[:, 0, :]


def _lru_step_kernel(rx_ref, rg_ref, cp_ref, h0_ref, cw_ref, cb_ref, wr_ref, wi_ref, br_ref, bi_ref, lam_ref,
                     y_ref, h_ref, *, blk):
    hist = _CONV_WIDTH - 1
    xc = cb_ref[...] + rx_ref[...] * cw_ref[pl.ds(hist, 1), :]
    for j in range(hist):
        xc = xc + cp_ref[j] * cw_ref[pl.ds(j, 1), :]
    a, b = _lru_gates(xc, wr_ref, wi_ref, br_ref, bi_ref, lam_ref, blk)
    h = a * h0_ref[...] + b
    h_ref[...] = h
    y_ref[...] = (h * _gelu_tanh(rg_ref[...])).astype(y_ref.dtype)


def _lru_step(z, rx_col, rg_col, width, conv_prev_t, h_prev, conv_w, conv_b3, wr, wi, br3, bi3, lam3, l):
    bd = z.shape[0]
    wb = _pick(width, 512)
    blk = wr.shape[-1]
    nb = wb // blk
    hist = _CONV_WIDTH - 1
    rxo, rgo = rx_col // wb, rg_col // wb
    vec = lambda: pl.BlockSpec((None, 1, wb), lambda w: (l, 0, w))
    gate_w = lambda: pl.BlockSpec((None, nb, blk, blk), lambda w: (l, w, 0, 0))
    return pl.pallas_call(
        functools.partial(_lru_step_kernel, blk=blk),
        name="rglru_step",
        grid=(width // wb,),
        in_specs=[
            pl.BlockSpec((bd, wb), lambda w: (0, rxo + w)),
            pl.BlockSpec((bd, wb), lambda w: (0, rgo + w)),
            pl.BlockSpec((hist, bd, wb), lambda w: (0, 0, w)),
            pl.BlockSpec((bd, wb), lambda w: (0, w)),
            pl.BlockSpec((None, _CONV_WIDTH, wb), lambda w: (l, 0, w)),
            vec(), gate_w(), gate_w(), vec(), vec(), vec(),
        ],
        out_specs=[pl.BlockSpec((bd, wb), lambda w: (0, w)), pl.BlockSpec((bd, wb), lambda w: (0, w))],
        out_shape=[jax.ShapeDtypeStruct((bd, width), _BF16), jax.ShapeDtypeStruct((bd, width), _F32)],
        compiler_params=pltpu.CompilerParams(dimension_semantics=("parallel",)),
    )(z, z, conv_prev_t, h_prev, conv_w, conv_b3, wr, wi, br3, bi3, lam3)


def _merge(o, y, w_att, w_lru, gates, l, tok):
    m, ko = o.shape
    ky = y.shape[1]
    d = w_att.shape[-1]
    tm = tok.tm
    assert ko == ky
    tk, tn = _mm_tiles(ko, d, n_w=2)
    nb = d // tn

    def epi(accs, ex, outs):
        outs[0][...] = (ex[0][...] * accs[0] + ex[1][...] * accs[1]).astype(_BF16)

    (out,) = _matmul(
        [(o, _a_spec(tm, tk)), (y, _a_spec(tm, tk))],
        [(w_att, _w_spec((l,), tk, tn)), (w_lru, _w_spec((l,), tk, tn))],
        [0, 1],
        [(gates, _row_spec(tm, tn)), (gates, _row_spec(tm, tn, lambda n: n + nb))],
        [(jax.ShapeDtypeStruct((m, d), _BF16), _row_spec(tm, tn))],
        grid=(tok.m_tiles, nb, ko // tk), tm=tm, tn=tn, tk=tk, epilogue=epi, name="branch_merge")
    return out


def _residual_mm(a, w, lead, res, gate, tok, *, name, k0=0, kn=None, partial_in=None, partial_out=False):
    m, ka = a.shape
    d = w.shape[-1]
    kn = ka if kn is None else kn
    tm = tok.tm
    tk, tn = _mm_tiles(kn, d, k0=k0)
    koff = k0 // tk
    ex = []
    if partial_in is not None:
        ex.append((partial_in, _row_spec(tm, tn)))
    if not partial_out:
        ex.append((res, _row_spec(tm, tn)))
        ex.append(tok.mod(gate, tn))

    def epi(accs, exr, outs):
        acc = accs[0]
        i = 0
        if partial_in is not None:
            acc = acc + exr[i][...]
            i += 1
        if partial_out:
            outs[0][...] = acc
        else:
            outs[0][...] = exr[i][...] + exr[i + 1][...] * acc

    (out,) = _matmul(
        [(a, _a_spec(tm, tk, k_of=lambda k: k + koff))],
        [(w, _w_spec(lead, tk, tn, k_of=lambda k: k + koff))],
        [0], ex,
        [(jax.ShapeDtypeStruct((m, d), _F32), _row_spec(tm, tn))],
        grid=(tok.m_tiles, d // tn, kn // tk), tm=tm, tn=tn, tk=tk, epilogue=epi, name=name)
    return out


def _swiglu_up(u, w1, w3, lead, tok):
    m, kdim = u.shape
    n = w1.shape[-1]
    tm = tok.tm
    tk, tn = _mm_tiles(kdim, n, n_w=2)
    n_of = lambda j: j

    def epi(accs, ex, outs):
        g = accs[0]
        outs[0][...] = (g * jax.nn.sigmoid(g) * accs[1]).astype(_BF16)

    (out,) = _matmul(
        [(u, _a_spec(tm, tk))],
        [(w1, _w_spec(lead, tk, tn, n_of=n_of)), (w3, _w_spec(lead, tk, tn, n_of=n_of))],
        [0, 0], [],
        [(jax.ShapeDtypeStruct((m, n), _BF16), _row_spec(tm, tn))],
        grid=(tok.m_tiles, n // tn, kdim // tk), tm=tm, tn=tn, tk=tk, epilogue=epi, name="swiglu_up")
    return out


def _split_cols(n, t):
    main = (n // t) * t
    return main, n - main


def _dense_ffn(u2, w1, w3, w2, li, res, gate, tok):
    n = w1.shape[-1]
    h = _swiglu_up(u2, w1, w3, (li,), tok)
    main, tail = _split_cols(n, _TK)
    if tail == 0 or main == 0:
        return _residual_mm(h, w2, (li,), res, gate, tok, name="ffn_down")
    part = _residual_mm(h, w2, (li,), res, gate, tok, k0=0, kn=main, partial_out=True, name="ffn_down_main")
    return _residual_mm(h, w2, (li,), res, gate, tok, k0=main, kn=tail, partial_in=part, name="ffn_down_tail")


_MOE_TILE = 1536
_MOE_SUB = 256


def _route_rank_kernel(*refs, n_groups, tiles):
    route_refs = refs[:n_groups]
    rank_refs = refs[n_groups:2 * n_groups]
    cnt_ref, carry = refs[2 * n_groups:]
    i = pl.program_id(0)

    @pl.when(i == 0)
    def _():
        carry[...] = jnp.zeros(carry.shape, _F32)

    def tile(route_ref, rank_ref):
        r = route_ref[...]
        tr, n_e = r.shape
        lane = lax.broadcasted_iota(jnp.int32, (tr, n_e), 1).astype(_F32)
        oh1 = jnp.where(lane == r[:, 0:1], 1.0, 0.0)
        oh2 = jnp.where(lane == r[:, 1:2], 1.0, 0.0)
        oh = oh1 + oh2
        tri = jnp.where(
            lax.broadcasted_iota(jnp.int32, (tr, tr), 1) <= lax.broadcasted_iota(jnp.int32, (tr, tr), 0),
            1.0, 0.0).astype(_BF16)
        cum = jnp.dot(tri, oh.astype(_BF16), preferred_element_type=_F32) + carry[...]
        excl = cum - oh
        rank1 = jnp.sum(oh1 * excl, axis=-1, keepdims=True)
        rank2 = jnp.sum(oh2 * excl, axis=-1, keepdims=True)
        rank_ref[...] = jnp.where(lane == 0.0, rank1, jnp.where(lane == 1.0, rank2, 0.0))
        carry[...] = cum[tr - 1:tr, :]
        cnt_ref[...] = cum[tr - 1:tr, :]

    first = 0
    for g in range(n_groups):
        @pl.when(jnp.logical_and(i >= first, i < first + tiles[g]))
        def _():
            tile(route_refs[g], rank_refs[g])
        first += tiles[g]


def _route_rank(routes):
    n_e = routes[0].shape[1]
    trs = [_pick(r.shape[0], _MOE_SUB) for r in routes]
    tiles = [r.shape[0] // t for r, t in zip(routes, trs)]
    firsts = [sum(tiles[:g]) for g in range(len(routes))]

    def spec(g):
        return pl.BlockSpec((trs[g], n_e), lambda i: (jnp.clip(i - firsts[g], 0, tiles[g] - 1), 0))

    specs = [spec(g) for g in range(len(routes))]
    res = pl.pallas_call(
        functools.partial(_route_rank_kernel, n_groups=len(routes), tiles=tuple(tiles)),
        name="moe_route_rank",
        grid=(sum(tiles),),
        in_specs=specs,
        out_specs=specs + [pl.BlockSpec((1, n_e), lambda i: (0, 0))],
        out_shape=[jax.ShapeDtypeStruct(r.shape, _F32) for r in routes] + [jax.ShapeDtypeStruct((1, n_e), _F32)],
        scratch_shapes=[pltpu.VMEM((1, n_e), _F32)],
        compiler_params=pltpu.CompilerParams(dimension_semantics=("arbitrary",)),
    )(*routes)
    return res[:-1], res[-1]


class _MoePlan:
    def __init__(self, routes, n_e):
        tm, sub = _MOE_TILE, _MOE_SUB
        m_all = sum(r.shape[0] for r in routes)
        self.n_tiles = (_TOP_K * m_all + n_e * (sub - 1)) // tm + n_e
        self.rows = self.n_tiles * tm
        padded_routes = [jnp.pad(r, ((0, (-r.shape[0]) % _LANES), (0, 0)), constant_values=-1.0) for r in routes]
        ranks, counts = _route_rank(padded_routes)
        ranks = [rk[:r.shape[0]] for rk, r in zip(ranks, routes)]
        counts = counts[0].astype(jnp.int32)
        padded = (counts + sub - 1) // sub * sub
        nt_e = (padded + tm - 1) // tm
        per_tile = jnp.maximum((padded // jnp.maximum(nt_e, 1) + sub - 1) // sub * sub, sub)
        t_end = jnp.cumsum(nt_e)
        t_start = t_end - nt_e
        used = t_end[-1]
        self.pos = []
        for route, rank in zip(routes, ranks):
            idx = route[:, :_TOP_K].astype(jnp.int32)
            rk = rank[:, :_TOP_K].astype(jnp.int32)
            self.pos.append((t_start[idx] + rk // per_tile[idx]) * tm + rk % per_tile[idx])
        ti = jnp.arange(self.n_tiles, dtype=jnp.int32)
        te = jnp.minimum(jnp.searchsorted(t_end, ti, side="right").astype(jnp.int32), n_e - 1)
        live = ti < used
        te_last = jnp.minimum(jnp.searchsorted(t_end, used - 1, side="right").astype(jnp.int32), n_e - 1)
        self.tile_expert = jnp.where(live, te, te_last)
        self.tile_valid = jnp.where(
            live, jnp.clip(padded[te] - (ti - t_start[te]) * per_tile[te], 0, per_tile[te]), 0)
        flat = jnp.concatenate([p_.reshape(-1) for p_ in self.pos])
        tok = jnp.repeat(jnp.arange(m_all, dtype=jnp.int32), _TOP_K)
        self.row_token = jnp.zeros((self.rows,), jnp.int32).at[flat].set(tok)
        per = tm // sub
        ci = jnp.arange(self.rows // sub, dtype=jnp.int32)
        self.chunk_valid = ((ci % per) * sub < self.tile_valid[ci // per]).astype(jnp.int32)


_GATHER_UNROLL = 8


def _dispatch_kernel(tok_ref, cv_ref, src_ref, o_ref, buf, sems, *, sub):
    c = pl.program_id(0)
    n_chunks = pl.num_programs(0)

    def issue(chunk, slot):
        def body(r, carry):
            tok = tok_ref[chunk * sub + r]
            pltpu.make_async_copy(src_ref.at[pl.ds(tok, 1), :], buf.at[slot, pl.ds(r, 1), :], sems.at[slot]).start()
            return carry
        lax.fori_loop(0, sub, body, 0, unroll=_GATHER_UNROLL)

    def await_all(slot):
        pltpu.make_async_copy(src_ref.at[pl.ds(0, sub), :], buf.at[slot], sems.at[slot]).wait()

    slot = lax.rem(c, 2)

    @pl.when(jnp.logical_and(c == 0, cv_ref[0] > 0))
    def _():
        issue(0, 0)

    nxt = jnp.minimum(c + 1, n_chunks - 1)

    @pl.when(jnp.logical_and(c + 1 < n_chunks, cv_ref[nxt] > 0))
    def _():
        issue(c + 1, 1 - slot)

    @pl.when(cv_ref[c] > 0)
    def _():
        await_all(slot)
        o_ref[...] = buf[slot].astype(o_ref.dtype)

    @pl.when(cv_ref[c] == 0)
    def _():
        o_ref[...] = jnp.zeros(o_ref.shape, o_ref.dtype)


def _moe_dispatch(u_all, plan):
    m, d = u_all.shape
    sub = _MOE_SUB
    assert m >= sub
    return pl.pallas_call(
        functools.partial(_dispatch_kernel, sub=sub),
        name="moe_dispatch",
        grid_spec=pltpu.PrefetchScalarGridSpec(
            num_scalar_prefetch=2,
            grid=(plan.rows // sub,),
            in_specs=[pl.BlockSpec(memory_space=pl.ANY)],
            out_specs=pl.BlockSpec((sub, d), lambda c, tok, cv: (c, 0)),
            scratch_shapes=[pltpu.VMEM((2, sub, d), u_all.dtype), pltpu.SemaphoreType.DMA((2,))]),
        out_shape=jax.ShapeDtypeStruct((plan.rows, d), _BF16),
        compiler_params=pltpu.CompilerParams(
            dimension_semantics=("arbitrary",),
            vmem_limit_bytes=_vmem_limit(8 * sub * d * 4)),
    )(plan.row_token, plan.chunk_valid, u_all)


def _gmm_body(te_ref, tv_ref, a_ref, *rest, n_w, n_ex, nk, tm, sub, epilogue):
    w_refs = rest[:n_w]
    ex_refs = rest[n_w:n_w + n_ex]
    o_ref = rest[n_w + n_ex]
    acc_refs = rest[n_w + n_ex + 1:]
    k = pl.program_id(2)
    valid = tv_ref[pl.program_id(0)]

    def accumulate(rows):
        def products():
            a = a_ref[rows, :]
            return [jnp.dot(a, w[...].astype(_BF16), preferred_element_type=_F32) for w in w_refs]

        if nk == 1:
            o_ref[rows, :] = epilogue(products(), ex_refs, rows).astype(o_ref.dtype)
            return

        @pl.when(k == 0)
        def _():
            for j in range(n_w):
                acc_refs[j][rows, :] = jnp.zeros((rows.size, acc_refs[j].shape[1]), _F32)

        parts = products()
        for j in range(n_w):
            acc_refs[j][rows, :] += parts[j]

        @pl.when(k == nk - 1)
        def _():
            full = [acc_refs[j][rows, :] for j in range(n_w)]
            o_ref[rows, :] = epilogue(full, ex_refs, rows).astype(o_ref.dtype)

    units = valid // sub
    start = jnp.int32(0)
    bit = 1 << ((tm // sub).bit_length() - 1)
    while bit >= 1:
        take = (units & bit) != 0
        size = bit * sub

        @pl.when(take)
        def _():
            accumulate(pl.ds(pl.multiple_of(start, sub), size))

        start = start + jnp.where(take, size, 0)
        bit //= 2

    @pl.when(k == nk - 1)
    def _():
        for s in range(tm // sub):
            @pl.when(s * sub >= valid)
            def _():
                o_ref[pl.ds(s * sub, sub), :] = jnp.zeros((sub, o_ref.shape[1]), o_ref.dtype)


def _gmm(a, w_list, mi, plan, n, ex_ops, epilogue, out_dtype, name):
    kdim = a.shape[1]
    tm, sub = _MOE_TILE, _MOE_SUB
    n_w = len(w_list)
    tk, tn = _mm_tiles(kdim, n, n_w=n_w)
    nk = kdim // tk
    live_k = lambda m, k, tv: jnp.where(tv[m] > 0, k, 0)
    a_spec = pl.BlockSpec((tm, tk), lambda m, j, k, te, tv: (m, live_k(m, k, tv)))
    w_spec = pl.BlockSpec((None, None, tk, tn),
                          lambda m, j, k, te, tv: (mi, te[m], live_k(m, k, tv), live_k(m, j, tv)))
    o_spec = pl.BlockSpec((tm, tn), lambda m, j, k, te, tv: (m, j))
    est = 2 * tm * tk * 2 + n_w * (2 * tk * tn * 4 + tk * tn * 2 + 2 * tm * tn * 4) + 4 * tm * tn * 4
    return pl.pallas_call(
        functools.partial(_gmm_body, n_w=n_w, n_ex=len(ex_ops), nk=nk, tm=tm, sub=sub, epilogue=epilogue),
        name=name,
        grid_spec=pltpu.PrefetchScalarGridSpec(
            num_scalar_prefetch=2,
            grid=(plan.n_tiles, n // tn, nk),
            in_specs=[a_spec] + [w_spec] * n_w + [s for _, s in ex_ops],
            out_specs=o_spec,
            scratch_shapes=[pltpu.VMEM((tm, tn), _F32) for _ in range(n_w)] if nk > 1 else []),
        out_shape=jax.ShapeDtypeStruct((plan.rows, n), out_dtype),
        compiler_params=pltpu.CompilerParams(
            dimension_semantics=("arbitrary", "arbitrary", "arbitrary"),
            vmem_limit_bytes=_vmem_limit(est)),
    )(plan.tile_expert, plan.tile_valid, a, *w_list, *[x for x, _ in ex_ops])


def _combine_kernel(pos_ref, y_ref, x_ref, g_ref, route_ref, o_ref, ybuf, sems, *, tc):
    i = pl.program_id(0)
    n_steps = pl.num_programs(0)

    def row_copy(step, slot, r, j):
        src = y_ref.at[pl.ds(pos_ref[_TOP_K * (step * tc + r) + j], 1), :]
        return pltpu.make_async_copy(src, ybuf.at[slot, j, pl.ds(r, 1), :], sems.at[slot])

    def issue(step, slot):
        def body(r, carry):
            for j in range(_TOP_K):
                row_copy(step, slot, r, j).start()
            return carry
        lax.fori_loop(0, tc, body, 0, unroll=min(tc, _GATHER_UNROLL))

    slot = lax.rem(i, 2)

    @pl.when(i == 0)
    def _():
        issue(0, 0)

    @pl.when(i + 1 < n_steps)
    def _():
        issue(i + 1, 1 - slot)

    for j in range(_TOP_K):
        pltpu.make_async_copy(y_ref.at[pl.ds(0, tc), :], ybuf.at[slot, j], sems.at[slot]).wait()
    route = route_ref[...]
    y = ybuf[slot, 0] * route[:, _TOP_K:_TOP_K + 1]
    for j in range(1, _TOP_K):
        y = y + ybuf[slot, j] * route[:, _TOP_K + j:_TOP_K + j + 1]
    o_ref[...] = x_ref[...] + g_ref[...] * y


def _moe_combine(y, pos, route, res, gate, tok):
    m, d = res.shape
    n_e = route.shape[1]
    tc = tok.m if tok.per_row else _pick(tok.seq, 256)
    rows = _Tokens(tok.groups, tok.seq, tc)
    g3, g_spec = rows.mod(gate, d)
    return pl.pallas_call(
        functools.partial(_combine_kernel, tc=tc),
        name="moe_combine",
        grid_spec=pltpu.PrefetchScalarGridSpec(
            num_scalar_prefetch=1,
            grid=(m // tc,),
            in_specs=[pl.BlockSpec(memory_space=pl.ANY),
                      pl.BlockSpec((tc, d), lambda i, p: (i, 0)),
                      pl.BlockSpec(g_spec.block_shape, lambda i, p, f=g_spec.index_map: f(i, 0, 0)),
                      pl.BlockSpec((tc, n_e), lambda i, p: (i, 0))],
            out_specs=pl.BlockSpec((tc, d), lambda i, p: (i, 0)),
            scratch_shapes=[pltpu.VMEM((2, _TOP_K, tc, d), _F32), pltpu.SemaphoreType.DMA((2,))]),
        out_shape=jax.ShapeDtypeStruct((m, d), _F32),
        compiler_params=pltpu.CompilerParams(
            dimension_semantics=("arbitrary",),
            vmem_limit_bytes=_vmem_limit(12 * tc * d * 4)),
    )(pos.reshape(-1), y, res, g3, route)


def _moe_ffn(u_groups, route_groups, w1, w3, w2, mi, res_groups, gate_groups, toks):
    n_e = w1.shape[1]
    n = w1.shape[-1]
    d = w2.shape[-1]
    plan = _MoePlan(route_groups, n_e)
    a_sorted = _moe_dispatch(jnp.concatenate(u_groups, axis=0), plan)

    def up_epi(accs, ex, rows):
        g = accs[0]
        return g * jax.nn.sigmoid(g) * accs[1]

    h = _gmm(a_sorted, [w1, w3], mi, plan, n, [], up_epi, _BF16, "moe_up")
    y = _gmm(h, [w2], mi, plan, d, [], lambda accs, ex, rows: accs[0], _F32, "moe_down")
    return [_moe_combine(y, pos, route, res, gate, tok)
            for pos, route, res, gate, tok in zip(plan.pos, route_groups, res_groups, gate_groups, toks)]


def kernel(x_prompt, x_sample, c_prompt, c_sample, cache_k, cache_v, state_conv, state_h, page_table, rel_bias, ada_w, ada_b, norm1_g, w_in, q_norm_g, k_norm_g, lam_q1, lam_k1, lam_q2, lam_k2, subln_g, conv_w, conv_b, lru_wr, lru_br, lru_wi, lru_bi, lru_lambda, w_att, w_lru, gate_w, gate_b, w_o, norm2_g, ffn_w1, ffn_w3, ffn_w2, router_w, router_b, moe_w1, moe_w3, moe_w2):
    depth = w_in.shape[0]
    bp, seq, d = x_prompt.shape
    bd = x_sample.shape[0]
    n_heads = cache_k.shape[3]
    dv = cache_v.shape[4]
    hd = q_norm_g.shape[-1]
    att_w = n_heads * dv
    lru_w = conv_w.shape[-1]
    page = cache_k.shape[2]
    assert x_sample.shape[1] == 1 and cache_k.shape[4] == 2 * hd == dv

    ptok = _Tokens(bp, seq, _pick(seq, 1024))
    stok = _Tokens(bd, 1, bd)
    t_att = _pick(seq, 512)

    vec3 = lambda a: a.reshape(a.shape[:-1] + (1, a.shape[-1]))
    ada_b3, norm1_g3, norm2_g3 = vec3(ada_b), vec3(norm1_g), vec3(norm2_g)
    q_norm_g3, k_norm_g3, subln_g3 = vec3(q_norm_g), vec3(k_norm_g), vec3(subln_g)
    conv_b3, lru_br3, lru_bi3, lam3 = vec3(conv_b), vec3(lru_br), vec3(lru_bi), vec3(lru_lambda)
    gate_b3, router_b3 = vec3(gate_b), vec3(router_b)
    lam_par = jnp.stack([lam_q1, lam_k1, lam_q2, lam_k2], axis=1)

    flash_bias = _bias_tiles(rel_bias, _flash_buckets(t_att))
    dec_bias = _bias_tiles(rel_bias, _decode_buckets(page))
    same_head = jnp.eye(n_heads, dtype=bool)[None, :, None, :]
    dec_bias = jnp.where(same_head, jnp.transpose(dec_bias, (1, 0, 2))[..., None], _NEG_INF)
    dec_bias = dec_bias.reshape(dec_bias.shape[0], n_heads, page * n_heads)

    c_all = jnp.concatenate([c_prompt, c_sample], axis=0)
    pad_rows = (-c_all.shape[0]) % _SUBLANES
    c_all = jnp.pad(c_all, ((0, pad_rows), (0, 0)))

    xp = x_prompt.reshape(bp * seq, d)
    xs = x_sample.reshape(bd, d)
    zeros_conv = jnp.zeros((bp, _CONV_WIDTH - 1, lru_w), x_prompt.dtype)
    zeros_h = jnp.zeros((bp, lru_w), state_h.dtype)
    sm_scale = hd ** -0.5
    rx_col, rg_col = 3 * att_w, 3 * att_w + lru_w

    outs = {k_: [] for k_ in ("kp", "vp", "cp", "hp", "ks", "vs", "cs", "hs")}
    for l in range(depth):
        lam_init = 0.8 - 0.6 * math.exp(-0.3 * l)
        mod = _ada(c_all, ada_w, ada_b3, l)
        mods_p = [mod[:bp, j * d:(j + 1) * d] for j in range(6)]
        mods_s = [mod[bp:bp + bd, j * d:(j + 1) * d] for j in range(6)]
        dense = l % 2 == 0
        li = l // 2

        def mixer(x, tok, mods):
            sh1, sc1, g1 = mods[0], mods[1], mods[2]
            u = _norm_mod(x, norm1_g3, l, sc1, sh1, tok)
            q = _proj(u, w_in, l, 0, att_w, tok, out_dtype=_BF16, norm_g3=q_norm_g3, scale=sm_scale,
                      name="proj_q")
            k = _proj(u, w_in, l, att_w, att_w, tok, out_dtype=_F32, norm_g3=k_norm_g3, name="proj_k")
            v = _proj(u, w_in, l, 2 * att_w, att_w, tok, out_dtype=_F32, name="proj_v")
            r = _proj(u, w_in, l, 3 * att_w, 2 * lru_w, tok, out_dtype=_F32, name="proj_lru")
            gates = _proj(u, gate_w, l, 0, 2 * d, tok, out_dtype=_F32, bias3=gate_b3, act="sigmoid",
                          name="proj_gates")
            return g1, q, k, v, r, gates

        def channel(x1s, toks, modss):
            if dense:
                res = []
                for x1, tok, mods in zip(x1s, toks, modss):
                    u2 = _norm_mod(x1, norm2_g3, l, mods[4], mods[3], tok)
                    res.append(_dense_ffn(u2, ffn_w1, ffn_w3, ffn_w2, li, x1, mods[5], tok))
                return res
            us, routes = [], []
            for x1, tok, mods in zip(x1s, toks, modss):
                u2, route = _norm_mod(x1, norm2_g3, l, mods[4], mods[3], tok, router=(router_w, router_b3, li))
                us.append(u2)
                routes.append(route)
            return _moe_ffn(us, routes, moe_w1, moe_w3, moe_w2, li, x1s, [m_[5] for m_ in modss], toks)

        g1, q, k, v, r, gates = mixer(xp, ptok, mods_p)
        o = _prompt_attention(q.reshape(bp, seq, att_w), k.reshape(bp, seq, att_w), v.reshape(bp, seq, att_w),
                              rel_bias, flash_bias, lam_par, subln_g3, l, lam_init, n_heads, t_att)
        r3 = r.reshape(bp, seq, 2 * lru_w)
        y, h_new = _lru_seq(r3, 0, lru_w, lru_w, zeros_conv, zeros_h, conv_w, conv_b3, lru_wr, lru_wi,
                            lru_br3, lru_bi3, lam3, l, _pick(seq, 256))
        conv_new = jnp.concatenate([zeros_conv, r3[:, -(_CONV_WIDTH - 1):, :lru_w]], axis=1)[:, -(_CONV_WIDTH - 1):]
        t_mix = _merge(o.reshape(bp * seq, att_w), y.reshape(bp * seq, lru_w), w_att, w_lru, gates, l, ptok)
        x1_p = _residual_mm(t_mix, w_o, (l,), xp, g1, ptok, name="out_proj")
        outs["kp"].append(k.reshape(bp, seq, n_heads, dv))
        outs["vp"].append(v.reshape(bp, seq, n_heads, dv))
        outs["cp"].append(conv_new)
        outs["hp"].append(h_new)

        g1, q, k, v, r, gates = mixer(xs, stok, mods_s)
        o = _sample_attention(q.reshape(bd, n_heads, dv), k.reshape(bd, n_heads, dv), v.reshape(bd, n_heads, dv),
                              cache_k, cache_v, page_table, dec_bias, lam_par, subln_g3, l, lam_init)
        conv_prev = state_conv[l]
        y, h_new = _lru_step(r, 0, lru_w, lru_w, jnp.swapaxes(conv_prev, 0, 1), state_h[l], conv_w, conv_b3,
                             lru_wr, lru_wi, lru_br3, lru_bi3, lam3, l)
        conv_new = jnp.concatenate([conv_prev, r[:, None, :lru_w]], axis=1)[:, -(_CONV_WIDTH - 1):]
        t_mix = _merge(o.reshape(bd, att_w), y, w_att, w_lru, gates, l, stok)
        x1_s = _residual_mm(t_mix, w_o, (l,), xs, g1, stok, name="out_proj")
        outs["ks"].append(k.reshape(bd, 1, n_heads, dv))
        outs["vs"].append(v.reshape(bd, 1, n_heads, dv))
        outs["cs"].append(conv_new)
        outs["hs"].append(h_new)

        xp, xs = channel([x1_p, x1_s], [ptok, stok], [mods_p, mods_s])

    st = jnp.stack
    return (xp.reshape(bp, seq, d), xs.reshape(bd, 1, d), st(outs["kp"]), st(outs["vp"]), st(outs["cp"]),
            st(outs["hp"]), st(outs["ks"]), st(outs["vs"]), st(outs["cs"]), st(outs["hs"]))
```

```python
import functools
import math

import numpy as np
import jax
import jax.numpy as jnp
from jax import lax
from jax.experimental import pallas as pl
from jax.experimental.pallas import tpu as pltpu

_F32 = jnp.float32
_BF16 = jnp.bfloat16

_EPS = 1e-6
_LRU_C = 8.0
_CONV_WIDTH = 4
_N_BUCKETS = 32
_MAX_DISTANCE = 128
_TOP_K = 2

_LANES = 128
_SUBLANES = 8
_VMEM_CAP_BYTES = 60 * 1024 * 1024
_VMEM_MIN_BYTES = 32 * 1024 * 1024

_NEG_INF = float("-inf")


def _vmem_limit(est_bytes):
    return int(min(_VMEM_CAP_BYTES, max(_VMEM_MIN_BYTES, est_bytes)))


def _nbytes(shape, dtype):
    return int(np.prod(shape)) * jnp.dtype(dtype).itemsize


def _pick(n, pref):
    t = pref
    while t >= _LANES:
        if n % t == 0:
            return t
        t //= 2
    return n


def _mm_body(*refs, n_a, w_a, n_ex, n_out, nk, a_fn, epilogue):
    n_w = len(w_a)
    a_refs = refs[:n_a]
    w_refs = refs[n_a:n_a + n_w]
    ex_refs = refs[n_a + n_w:n_a + n_w + n_ex]
    out_refs = refs[n_a + n_w + n_ex:n_a + n_w + n_ex + n_out]
    acc_refs = refs[n_a + n_w + n_ex + n_out:]
    def products():
        a_vals = [a_fn(r[...]) for r in a_refs]
        return [jnp.dot(a_vals[w_a[j]], w_refs[j][...].astype(_BF16), preferred_element_type=_F32)
                for j in range(n_w)]

    if nk == 1:
        epilogue(products(), ex_refs, out_refs)
        return
    k = pl.program_id(2)

    @pl.when(k == 0)
    def _():
        parts = products()
        for j in range(n_w):
            acc_refs[j][...] = parts[j]

    @pl.when(jnp.logical_and(k > 0, k < nk - 1))
    def _():
        parts = products()
        for j in range(n_w):
            acc_refs[j][...] += parts[j]

    @pl.when(k == nk - 1)
    def _():
        parts = products()
        epilogue([acc_refs[j][...] + parts[j] for j in range(n_w)], ex_refs, out_refs)


def _to_bf16(x):
    return x.astype(_BF16)


_TM = 1024
_FULL_K = 4096
_TN_FULL_K = 512
_TN_TILED_K = 1024
_TK = 2048


def _mm_tiles(kdim, n, n_w=1, k0=0, n0=0):
    if kdim <= _FULL_K and k0 == 0:
        tn_pref = max(2 * _LANES, min(_TN_TILED_K, _TN_FULL_K * _FULL_K // (kdim * n_w)))
        return kdim, _pick(math.gcd(n, n0), 1 << (tn_pref.bit_length() - 1))
    return _pick(math.gcd(kdim, k0), _TK), _pick(math.gcd(n, n0), _TN_TILED_K)


def _matmul(a_ops, w_ops, w_a, ex_ops, outs, *, grid, tm, tn, tk, epilogue, name, a_fn=_to_bf16):
    nk = grid[2]
    n_w = len(w_ops)
    scratch = [pltpu.VMEM((tm, tn), _F32) for _ in range(n_w)] if nk > 1 else []
    est = 0
    for arr, spec in a_ops + w_ops + ex_ops:
        blk = [d for d in spec.block_shape if d is not None]
        est += 2 * _nbytes(blk, arr.dtype)
    for sds, spec in outs:
        blk = [d for d in spec.block_shape if d is not None]
        est += 2 * _nbytes(blk, sds.dtype)
    est += n_w * (2 * tm * tn * 4 + tk * tn * 2) + 2 * tm * tn * 4
    body = functools.partial(
        _mm_body, n_a=len(a_ops), w_a=tuple(w_a), n_ex=len(ex_ops), n_out=len(outs),
        nk=nk, a_fn=a_fn, epilogue=epilogue)
    res = pl.pallas_call(
        body,
        name=name,
        grid=grid,
        in_specs=[s for _, s in a_ops + w_ops + ex_ops],
        out_specs=[s for _, s in outs],
        out_shape=[s for s, _ in outs],
        scratch_shapes=scratch,
        compiler_params=pltpu.CompilerParams(
            dimension_semantics=("parallel", "parallel", "arbitrary"),
            vmem_limit_bytes=_vmem_limit(est)),
    )(*[a for a, _ in a_ops + w_ops + ex_ops])
    return res


class _Tokens:
    def __init__(self, groups, seq, tm):
        self.groups, self.seq, self.tm = groups, seq, tm
        self.m = groups * seq
        if seq % tm == 0:
            self.per_row = False
        else:
            assert seq == 1 and tm == self.m
            self.per_row = True
        self.m_tiles = self.m // tm

    def mod(self, arr2d, tn, n_of=lambda n: n):
        g, d = arr2d.shape
        assert g == self.groups
        if self.per_row:
            a3 = arr2d.reshape(1, g, d)
            return a3, pl.BlockSpec((None, g, tn), lambda m, n, k: (0, 0, n_of(n)))
        a3 = arr2d.reshape(g, 1, d)
        per = self.seq // self.tm
        return a3, pl.BlockSpec((None, 1, tn), lambda m, n, k: (m // per, 0, n_of(n)))


def _row_spec(tm, tn, n_of=lambda n: n):
    return pl.BlockSpec((tm, tn), lambda m, n, k: (m, n_of(n)))


def _a_spec(tm, tk, k_of=lambda k: k):
    return pl.BlockSpec((tm, tk), lambda m, n, k: (m, k_of(k)))


def _w_spec(lead, tk, tn, k_of=lambda k: k, n_of=lambda n: n):
    nl = len(lead)
    return pl.BlockSpec((None,) * nl + (tk, tn), lambda m, n, k: tuple(lead) + (k_of(k), n_of(n)))


def _vec_spec(lead, tn, n_of=lambda n: n):
    nl = len(lead)
    return pl.BlockSpec((None,) * nl + (1, tn), lambda m, n, k: tuple(lead) + (0, n_of(n)))


def _ada(c_all, ada_w, ada_b3, l):
    m, d = c_all.shape
    n = ada_w.shape[-1]
    tn = _pick(n, 512)

    def a_fn(c):
        return (c * jax.nn.sigmoid(c)).astype(_BF16)

    def epi(accs, ex, outs):
        outs[0][...] = accs[0] + ex[0][...]

    (out,) = _matmul(
        [(c_all, pl.BlockSpec((m, d), lambda i, n_, k: (0, 0)))],
        [(ada_w, _w_spec((l,), d, tn))],
        [0],
        [(ada_b3, _vec_spec((l,), tn))],
        [(jax.ShapeDtypeStruct((m, n), _F32), pl.BlockSpec((m, tn), lambda i, n_, k: (0, n_)))],
        grid=(1, n // tn, 1), tm=m, tn=tn, tk=d, epilogue=epi, a_fn=a_fn, name="ada_mod")
    return out


def _norm_mod_kernel(x_ref, g_ref, sc_ref, sh_ref, *rest, router):
    x = x_ref[...]
    y = x * lax.rsqrt(jnp.mean(x * x, axis=-1, keepdims=True) + _EPS) * g_ref[...]
    u = y * (1.0 + sc_ref[...]) + sh_ref[...]
    if not router:
        (o_ref,) = rest
        o_ref[...] = u.astype(o_ref.dtype)
        return
    rw_ref, rb_ref, o_ref, gate_ref = rest
    o_ref[...] = u.astype(o_ref.dtype)
    n_e = rb_ref.shape[-1]
    logits = jnp.dot(u, rw_ref[...], preferred_element_type=_F32,
                     precision=lax.Precision.HIGHEST) + rb_ref[...]
    lane = lax.broadcasted_iota(jnp.int32, logits.shape, 1).astype(_F32)
    v1 = jnp.max(logits, axis=-1, keepdims=True)
    i1 = jnp.min(jnp.where(logits == v1, lane, float(n_e)), axis=-1, keepdims=True)
    rest_l = jnp.where(lane == i1, _NEG_INF, logits)
    v2 = jnp.max(rest_l, axis=-1, keepdims=True)
    i2 = jnp.min(jnp.where(rest_l == v2, lane, float(n_e)), axis=-1, keepdims=True)
    e2 = jnp.exp(v2 - v1)
    w1 = 1.0 / (1.0 + e2)
    w2 = e2 / (1.0 + e2)
    gate_ref[...] = jnp.where(lane == 0.0, i1, jnp.where(lane == 1.0, i2, jnp.where(
        lane == 2.0, w1, jnp.where(lane == 3.0, w2, 0.0))))


def _norm_mod(x, g3, l, sc, sh, tok, router=None):
    m, d = x.shape
    tm = tok.m if tok.per_row else _pick(tok.seq, 256)
    rows = _Tokens(tok.groups, tok.seq, tm)
    sc3, sc_spec = rows.mod(sc, d)
    sh3, sh_spec = rows.mod(sh, d)
    fix = lambda spec: pl.BlockSpec(spec.block_shape, lambda i, f=spec.index_map: f(i, 0, 0))
    in_ops = [
        (x, pl.BlockSpec((tm, d), lambda i: (i, 0))),
        (g3, pl.BlockSpec((None, 1, d), lambda i: (l, 0, 0))),
        (sc3, fix(sc_spec)),
        (sh3, fix(sh_spec)),
    ]
    u_dtype = _BF16 if router is None else _F32
    outs = [(jax.ShapeDtypeStruct((m, d), u_dtype), pl.BlockSpec((tm, d), lambda i: (i, 0)))]
    if router is not None:
        rw, rb3, mi = router
        n_e = rw.shape[-1]
        in_ops += [
            (rw, pl.BlockSpec((None, d, n_e), lambda i: (mi, 0, 0))),
            (rb3, pl.BlockSpec((None, 1, n_e), lambda i: (mi, 0, 0))),
        ]
        outs.append((jax.ShapeDtypeStruct((m, n_e), _F32), pl.BlockSpec((tm, n_e), lambda i: (i, 0))))
    res = pl.pallas_call(
        functools.partial(_norm_mod_kernel, router=router is not None),
        name="norm_mod_route" if router is not None else "norm_mod",
        grid=(m // tm,),
        in_specs=[s for _, s in in_ops],
        out_specs=[s for _, s in outs],
        out_shape=[s for s, _ in outs],
        compiler_params=pltpu.CompilerParams(
            dimension_semantics=("parallel",),
            vmem_limit_bytes=_vmem_limit(8 * tm * d * 4)),
    )(*[a for a, _ in in_ops])
    return res if router is not None else res[0]


def _proj(u, w, l, col0, n, tok, *, out_dtype, name, norm_g3=None, scale=1.0, bias3=None, act=None):
    m, kdim = u.shape
    tm = tok.tm
    tk, tn = _mm_tiles(kdim, n, n0=col0)
    off = col0 // tn
    n_of = lambda j: j + off
    ex = []
    if norm_g3 is not None:
        hd = norm_g3.shape[-1]
        ex.append((norm_g3, pl.BlockSpec((None, 1, hd), lambda i, j, k: (l, 0, 0))))
    if bias3 is not None:
        ex.append((bias3, _vec_spec((l,), tn, n_of)))

    def epi(accs, exr, outs):
        acc = accs[0]
        if bias3 is not None:
            acc = acc + exr[-1][...]
        if norm_g3 is not None:
            g = exr[0][...] * scale
            hd_ = g.shape[-1]
            for c in range(tn // hd_):
                blk = acc[:, c * hd_:(c + 1) * hd_]
                y = blk * lax.rsqrt(jnp.mean(blk * blk, axis=-1, keepdims=True) + _EPS) * g
                outs[0][:, c * hd_:(c + 1) * hd_] = y.astype(out_dtype)
            return
        if act == "sigmoid":
            acc = jax.nn.sigmoid(acc)
        outs[0][...] = acc.astype(out_dtype)

    (out,) = _matmul(
        [(u, _a_spec(tm, tk))],
        [(w, _w_spec((l,), tk, tn, n_of=n_of))],
        [0], ex,
        [(jax.ShapeDtypeStruct((m, n), out_dtype), _row_spec(tm, tn))],
        grid=(tok.m_tiles, n // tn, kdim // tk), tm=tm, tn=tn, tk=tk, epilogue=epi, name=name)
    return out


def _bucket_of_distance(n):
    n = np.maximum(n, 0)
    max_exact = _N_BUCKETS // 2
    nf = np.maximum(n, 1).astype(np.float32)
    large = max_exact + (np.log(nf / np.float32(max_exact)) / np.float32(math.log(_MAX_DISTANCE / max_exact))
                         * np.float32(_N_BUCKETS - max_exact)).astype(np.int32)
    large = np.minimum(large, _N_BUCKETS - 1)
    return np.where(n < max_exact, n, large).astype(np.int32)


def _bias_kernel(rel_ref, bkt_ref, o_ref):
    h = pl.program_id(0)
    b = bkt_ref[...]
    tile = jnp.zeros(b.shape, _F32)
    for j in range(_N_BUCKETS):
        tile = jnp.where(b == j, rel_ref[j, h], tile)
    o_ref[...] = jnp.where(b < 0, _NEG_INF, tile)


def _bias_tiles(rel_bias, buckets):
    n_h = rel_bias.shape[1]
    shp = buckets.shape
    zeros = (0,) * len(shp)
    return pl.pallas_call(
        _bias_kernel,
        name="rel_bias_tiles",
        grid=(n_h,),
        in_specs=[pl.BlockSpec(memory_space=pltpu.SMEM),
                  pl.BlockSpec(shp, lambda h: zeros)],
        out_specs=pl.BlockSpec((None,) + shp, lambda h: (h,) + zeros),
        out_shape=jax.ShapeDtypeStruct((n_h,) + shp, _F32),
    )(rel_bias, jnp.asarray(buckets))


_FAR_BUCKET = int(_bucket_of_distance(np.array(_MAX_DISTANCE)))


def _flash_buckets(t):
    assert t >= _MAX_DISTANCE
    r = np.arange(t)[:, None]
    c = np.arange(t)[None, :]
    prev = _bucket_of_distance(t + r - c)
    diag = np.where(c <= r, _bucket_of_distance(r - c), -1)
    return np.stack([prev, diag]).astype(np.int32)


def _decode_buckets(page):
    assert page >= _MAX_DISTANCE
    r = np.arange(page)
    far = np.full((page,), _FAR_BUCKET, np.int32)
    last = _bucket_of_distance(page - r)
    new = np.where(r == 0, _bucket_of_distance(np.array(0)), -1)
    return np.stack([far, last, new]).astype(np.int32)


def _lam_value(lam_ref, lam_init):
    lp = lam_ref[...]
    s1 = jnp.sum(lp[0:1] * lp[1:2], axis=-1, keepdims=True)
    s2 = jnp.sum(lp[2:3] * lp[3:4], axis=-1, keepdims=True)
    return jnp.exp(s1) - jnp.exp(s2) + lam_init


def _softmax_init(m_sc, l_sc, acc_sc):
    m_sc[...] = jnp.full(m_sc.shape, _NEG_INF, _F32)
    l_sc[...] = jnp.zeros(l_sc.shape, _F32)
    acc_sc[...] = jnp.zeros(acc_sc.shape, _F32)


def _softmax_update(q, kb, vb, bias, m_sc, l_sc, acc_sc, hd):
    for mi in range(2):
        s = lax.dot_general(q[:, mi * hd:(mi + 1) * hd], kb[:, mi * hd:(mi + 1) * hd],
                            (((1,), (1,)), ((), ())), preferred_element_type=_F32) + bias
        m_old = m_sc[mi]
        m_new = jnp.maximum(m_old, jnp.max(s, axis=-1, keepdims=True))
        alpha = jnp.exp(m_old - m_new)
        p = jnp.exp(s - m_new)
        l_sc[mi] = alpha * l_sc[mi] + jnp.sum(p, axis=-1, keepdims=True)
        acc_sc[mi] = alpha * acc_sc[mi] + jnp.dot(p.astype(_BF16), vb, preferred_element_type=_F32)
        m_sc[mi] = m_new


def _diff_finalize(lam_ref, g_ref, o_ref, l_sc, acc_sc, lam_init):
    lam = _lam_value(lam_ref, lam_init)
    o = acc_sc[0] / l_sc[0] - lam * (acc_sc[1] / l_sc[1])
    y = o * lax.rsqrt(jnp.mean(o * o, axis=-1, keepdims=True) + _EPS) * g_ref[...]
    o_ref[...] = (y * (1.0 - lam_init)).astype(o_ref.dtype)


def _flash_kernel(qi_ref, ki_ref, rel_ref, lam_ref, g_ref, q_ref, k_ref, v_ref, bias_ref, o_ref,
                  m_sc, l_sc, acc_sc, *, hd, lam_init):
    h = pl.program_id(1)
    step = pl.program_id(2)
    qi = qi_ref[step]
    ki = ki_ref[step]

    @pl.when(ki == 0)
    def _():
        _softmax_init(m_sc, l_sc, acc_sc)

    q = q_ref[...]
    kb = k_ref[...].astype(_BF16)
    vb = v_ref[...].astype(_BF16)

    @pl.when(ki < qi - 1)
    def _():
        _softmax_update(q, kb, vb, rel_ref[_FAR_BUCKET, h], m_sc, l_sc, acc_sc, hd)

    @pl.when(ki >= qi - 1)
    def _():
        _softmax_update(q, kb, vb, bias_ref[...], m_sc, l_sc, acc_sc, hd)

    @pl.when(ki == qi)
    def _():
        _diff_finalize(lam_ref, g_ref, o_ref, l_sc, acc_sc, lam_init)


def _prompt_attention(q, k, v, rel_bias, bias, lam_par, subln_g3, l, lam_init, n_heads, t):
    b, s, width = q.shape
    dv = width // n_heads
    hd = dv // 2
    nq = s // t
    pairs = [(qi, ki) for qi in range(nq) for ki in range(qi + 1)]
    qi_tab = jnp.asarray([p_[0] for p_ in pairs], jnp.int32)
    ki_tab = jnp.asarray([p_[1] for p_ in pairs], jnp.int32)
    q_map = lambda b_, h, p, qt, kt: (b_, qt[p], h)
    kv_map = lambda b_, h, p, qt, kt: (b_, kt[p], h)
    bias_map = lambda b_, h, p, qt, kt: (h, jnp.where(kt[p] == qt[p], 1, 0), 0, 0)
    par = lambda b_, h, p, qt, kt: (l, 0, 0)
    return pl.pallas_call(
        functools.partial(_flash_kernel, hd=hd, lam_init=lam_init),
        name="prompt_attention",
        grid_spec=pltpu.PrefetchScalarGridSpec(
            num_scalar_prefetch=2,
            grid=(b, n_heads, len(pairs)),
            in_specs=[
                pl.BlockSpec(memory_space=pltpu.SMEM),
                pl.BlockSpec((None, 4, hd), par),
                pl.BlockSpec((None, 1, dv), par),
                pl.BlockSpec((None, t, dv), q_map),
                pl.BlockSpec((None, t, dv), kv_map),
                pl.BlockSpec((None, t, dv), kv_map),
                pl.BlockSpec((None, None, t, t), bias_map),
            ],
            out_specs=pl.BlockSpec((None, t, dv), q_map),
            scratch_shapes=[pltpu.VMEM((2, t, 1), _F32), pltpu.VMEM((2, t, 1), _F32),
                            pltpu.VMEM((2, t, dv), _F32)]),
        out_shape=jax.ShapeDtypeStruct((b, s, width), _BF16),
        compiler_params=pltpu.CompilerParams(
            dimension_semantics=("parallel", "parallel", "arbitrary"),
            vmem_limit_bytes=_vmem_limit(24 * t * t * 4)),
    )(qi_tab, ki_tab, rel_bias, lam_par, subln_g3, q, k, v, bias)


def _block_softmax(q, kb, vb, bias, hd):
    out = []
    for mi in range(2):
        s = lax.dot_general(q[:, mi * hd:(mi + 1) * hd], kb[:, mi * hd:(mi + 1) * hd],
                            (((1,), (1,)), ((), ())), preferred_element_type=_F32) + bias
        m = jnp.max(s, axis=-1, keepdims=True)
        p = jnp.exp(s - m)
        out.append((m, jnp.sum(p, axis=-1, keepdims=True),
                    jnp.dot(p.astype(_BF16), vb, preferred_element_type=_F32)))
    return out


def _softmax_merge(blocks, m_sc, l_sc, acc_sc):
    for mi in range(2):
        m_old = m_sc[mi]
        m_new = m_old
        for blk in blocks:
            m_new = jnp.maximum(m_new, blk[mi][0])
        alpha = jnp.exp(m_old - m_new)
        l_new = alpha * l_sc[mi]
        acc_new = alpha * acc_sc[mi]
        for blk in blocks:
            w = jnp.exp(blk[mi][0] - m_new)
            l_new = l_new + w * blk[mi][1]
            acc_new = acc_new + w * blk[mi][2]
        m_sc[mi] = m_new
        l_sc[mi] = l_new
        acc_sc[mi] = acc_new


def _decode_kernel(pt_ref, lam_ref, g_ref, q_ref, kn_ref, vn_ref, *rest, hd, n_steps, group, lam_init):
    kc_refs = rest[:group]
    vc_refs = rest[group:2 * group]
    bias_refs = rest[2 * group:3 * group]
    bias_new_ref, o_ref, m_sc, l_sc, acc_sc = rest[3 * group:]
    p = pl.program_id(1)

    @pl.when(p == 0)
    def _():
        _softmax_init(m_sc, l_sc, acc_sc)

    def block(k_ref, v_ref, bias):
        r, nh, width = k_ref.shape
        kb = k_ref[...].reshape(r * nh, width).astype(_BF16)
        vb = v_ref[...].reshape(r * nh, width).astype(_BF16)
        return _block_softmax(q_ref[...], kb, vb, bias, hd)

    @pl.when(p < n_steps)
    def _():
        blocks = [block(kc_refs[g], vc_refs[g], bias_refs[g][...]) for g in range(group)]
        _softmax_merge(blocks, m_sc, l_sc, acc_sc)

    @pl.when(p == n_steps)
    def _():
        n_new = kn_ref.shape[0] * kn_ref.shape[1]
        _softmax_merge([block(kn_ref, vn_ref, bias_new_ref[:, :n_new])], m_sc, l_sc, acc_sc)
        _diff_finalize(lam_ref, g_ref, o_ref, l_sc, acc_sc, lam_init)


_DECODE_PAGES_PER_STEP = 4


def _sample_attention(q, k_new, v_new, cache_k, cache_v, page_table, bias, lam_par, subln_g3, l, lam_init):
    bd, n_heads, dv = q.shape
    hd = dv // 2
    n_pages = page_table.shape[1]
    page = cache_k.shape[2]
    group = max(g for g in range(1, _DECODE_PAGES_PER_STEP + 1) if n_pages % g == 0)
    n_steps = n_pages // group
    pad = ((0, 0), (0, _SUBLANES - 1), (0, 0), (0, 0))
    kn = jnp.pad(k_new[:, None], pad)
    vn = jnp.pad(v_new[:, None], pad)
    last = n_pages - 1

    def page_of(p, g):
        return jnp.minimum(p, n_steps - 1) * group + g

    def cache_spec(g):
        return pl.BlockSpec((None, None, page, n_heads, dv),
                            lambda b_, p, pt: (l, pt[b_, page_of(p, g)], 0, 0, 0))

    def bias_spec(g):
        return pl.BlockSpec((None, n_heads, page * n_heads),
                            lambda b_, p, pt: (jnp.where(page_of(p, g) < last, 0, 1), 0, 0))

    par = lambda b_, p, pt: (l, 0, 0)
    grid_spec = pltpu.PrefetchScalarGridSpec(
        num_scalar_prefetch=1,
        grid=(bd, n_steps + 1),
        in_specs=[
            pl.BlockSpec((None, 4, hd), par),
            pl.BlockSpec((None, 1, dv), par),
            pl.BlockSpec((None, n_heads, dv), lambda b_, p, pt: (b_, 0, 0)),
            pl.BlockSpec((None, _SUBLANES, n_heads, dv), lambda b_, p, pt: (b_, 0, 0, 0)),
            pl.BlockSpec((None, _SUBLANES, n_heads, dv), lambda b_, p, pt: (b_, 0, 0, 0)),
        ] + [cache_spec(g) for g in range(group)] * 2 + [bias_spec(g) for g in range(group)] + [
            pl.BlockSpec((None, n_heads, page * n_heads), lambda b_, p, pt: (2, 0, 0)),
        ],
        out_specs=pl.BlockSpec((None, n_heads, dv), lambda b_, p, pt: (b_, 0, 0)),
        scratch_shapes=[pltpu.VMEM((2, n_heads, 1), _F32), pltpu.VMEM((2, n_heads, 1), _F32),
                        pltpu.VMEM((2, n_heads, dv), _F32)],
    )
    return pl.pallas_call(
        functools.partial(_decode_kernel, hd=hd, n_steps=n_steps, group=group, lam_init=lam_init),
        name="sample_attention",
        grid_spec=grid_spec,
        out_shape=jax.ShapeDtypeStruct((bd, n_heads, dv), _BF16),
        compiler_params=pltpu.CompilerParams(
            dimension_semantics=("parallel", "arbitrary"),
            vmem_limit_bytes=_vmem_limit(8 * group * page * n_heads * dv * 4)),
    )(page_table, lam_par, subln_g3, q, kn, vn, *([cache_k] * group), *([cache_v] * group),
      *([bias] * (group + 1)))


def _gelu_tanh(x):
    return 0.5 * x * (1.0 + jnp.tanh(math.sqrt(2.0 / math.pi) * (x + 0.044715 * (x * x * x))))


def _softplus(x):
    return jnp.maximum(x, 0.0) + jnp.log1p(jnp.exp(-jnp.abs(x)))


def _lru_gates(xc, wr_ref, wi_ref, br_ref, bi_ref, lam_ref, blk):
    xcb = xc.astype(_BF16)
    r_parts, i_parts = [], []
    for j in range(xc.shape[1] // blk):
        xj = xcb[:, j * blk:(j + 1) * blk]
        r_parts.append(jnp.dot(xj, wr_ref[j].astype(_BF16), preferred_element_type=_F32))
        i_parts.append(jnp.dot(xj, wi_ref[j].astype(_BF16), preferred_element_type=_F32))
    r = jax.nn.sigmoid(jnp.concatenate(r_parts, axis=-1) + br_ref[...])
    i = jax.nn.sigmoid(jnp.concatenate(i_parts, axis=-1) + bi_ref[...])
    log_a = -_LRU_C * r * _softplus(-lam_ref[...])
    a = jnp.exp(log_a)
    b = jnp.sqrt(-jnp.tanh(log_a) * (a * a + 1.0)) * (i * xc)
    return a, b


def _lru_seq_kernel(rx_ref, rg_ref, cp_ref, h0_ref, cw_ref, cb_ref, wr_ref, wi_ref, br_ref, bi_ref, lam_ref,
                    y_ref, hl_ref, xbuf, a_sc, b_sc, h_sc, *, tt, blk):
    t = pl.program_id(2)
    nt = pl.num_programs(2)
    hist = _CONV_WIDTH - 1
    base = _SUBLANES

    @pl.when(t == 0)
    def _():
        xbuf[pl.ds(base - hist, hist), :] = cp_ref[...]
        h_sc[...] = h0_ref[...]

    @pl.when(t > 0)
    def _():
        xbuf[pl.ds(base - hist, hist), :] = xbuf[pl.ds(base + tt - hist, hist), :]

    xbuf[pl.ds(base, tt), :] = rx_ref[...]
    xc = cb_ref[...] + sum(xbuf[pl.ds(base - hist + j, tt), :] * cw_ref[pl.ds(j, 1), :]
                           for j in range(_CONV_WIDTH))
    a, b = _lru_gates(xc, wr_ref, wi_ref, br_ref, bi_ref, lam_ref, blk)
    a_sc[...] = a
    b_sc[...] = b
    row = lax.broadcasted_iota(jnp.int32, (_SUBLANES, a.shape[1]), 0)

    def group(gi, h):
        r0 = pl.multiple_of(gi * _SUBLANES, _SUBLANES)
        ag = a_sc[pl.ds(r0, _SUBLANES), :]
        bg = b_sc[pl.ds(r0, _SUBLANES), :]
        for sft in (1, 2, 4):
            a_prev = pltpu.roll(ag, sft, axis=0)
            b_prev = pltpu.roll(bg, sft, axis=0)
            keep = row >= sft
            bg = jnp.where(keep, ag * b_prev + bg, bg)
            ag = jnp.where(keep, ag * a_prev, ag)
        hg = ag * h + bg
        b_sc[pl.ds(r0, _SUBLANES), :] = hg
        return jnp.broadcast_to(hg[_SUBLANES - 1:_SUBLANES, :], hg.shape)

    h_in = jnp.broadcast_to(h_sc[...], (_SUBLANES, a.shape[1]))
    h_out = lax.fori_loop(0, tt // _SUBLANES, group, h_in)
    h_sc[...] = h_out[0:1, :]
    y_ref[...] = (b_sc[...] * _gelu_tanh(rg_ref[...])).astype(y_ref.dtype)

    @pl.when(t == nt - 1)
    def _():
        hl_ref[...] = h_out[0:1, :]


def _lru_seq(z, rx_col, rg_col, width, conv_prev, h_prev, conv_w, conv_b3, wr, wi, br3, bi3, lam3, l, tt):
    b, s, _ = z.shape
    wb = _pick(width, 512)
    blk = wr.shape[-1]
    nb = wb // blk
    hist = _CONV_WIDTH - 1
    assert rx_col % wb == 0 and rg_col % wb == 0
    rxo, rgo = rx_col // wb, rg_col // wb
    vec = lambda: pl.BlockSpec((None, 1, wb), lambda b_, w, t: (l, 0, w))
    gate_w = lambda: pl.BlockSpec((None, nb, blk, blk), lambda b_, w, t: (l, w, 0, 0))
    y, h_last = pl.pallas_call(
        functools.partial(_lru_seq_kernel, tt=tt, blk=blk),
        name="rglru_seq",
        grid=(b, width // wb, s // tt),
        in_specs=[
            pl.BlockSpec((None, tt, wb), lambda b_, w, t: (b_, t, rxo + w)),
            pl.BlockSpec((None, tt, wb), lambda b_, w, t: (b_, t, rgo + w)),
            pl.BlockSpec((None, hist, wb), lambda b_, w, t: (b_, 0, w)),
            pl.BlockSpec((None, 1, wb), lambda b_, w, t: (b_, 0, w)),
            pl.BlockSpec((None, _CONV_WIDTH, wb), lambda b_, w, t: (l, 0, w)),
            vec(), gate_w(), gate_w(), vec(), vec(), vec(),
        ],
        out_specs=[
            pl.BlockSpec((None, tt, wb), lambda b_, w, t: (b_, t, w)),
            pl.BlockSpec((None, 1, wb), lambda b_, w, t: (b_, 0, w)),
        ],
        out_shape=[jax.ShapeDtypeStruct((b, s, width), _BF16),
                   jax.ShapeDtypeStruct((b, 1, width), _F32)],
        scratch_shapes=[pltpu.VMEM((tt + _SUBLANES, wb), _F32), pltpu.VMEM((tt, wb), _F32),
                        pltpu.VMEM((tt, wb), _F32), pltpu.VMEM((1, wb), _F32)],
        compiler_params=pltpu.CompilerParams(
            dimension_semantics=("parallel", "parallel", "arbitrary"),
            vmem_limit_bytes=_vmem_limit(24 * tt * wb * 4)),
    )(z, z, conv_prev, h_prev[:, None, :], conv_w, conv_b3, wr, wi, br3, bi3, lam3)
    return y, h_last[:, 0, :]


def _lru_step_kernel(rx_ref, rg_ref, cp_ref, h0_ref, cw_ref, cb_ref, wr_ref, wi_ref, br_ref, bi_ref, lam_ref,
                     y_ref, h_ref, *, blk):
    hist = _CONV_WIDTH - 1
    xc = cb_ref[...] + rx_ref[...] * cw_ref[pl.ds(hist, 1), :]
    for j in range(hist):
        xc = xc + cp_ref[j] * cw_ref[pl.ds(j, 1), :]
    a, b = _lru_gates(xc, wr_ref, wi_ref, br_ref, bi_ref, lam_ref, blk)
    h = a * h0_ref[...] + b
    h_ref[...] = h
    y_ref[...] = (h * _gelu_tanh(rg_ref[...])).astype(y_ref.dtype)


def _lru_step(z, rx_col, rg_col, width, conv_prev_t, h_prev, conv_w, conv_b3, wr, wi, br3, bi3, lam3, l):
    bd = z.shape[0]
    wb = _pick(width, 512)
    blk = wr.shape[-1]
    nb = wb // blk
    hist = _CONV_WIDTH - 1
    rxo, rgo = rx_col // wb, rg_col // wb
    vec = lambda: pl.BlockSpec((None, 1, wb), lambda w: (l, 0, w))
    gate_w = lambda: pl.BlockSpec((None, nb, blk, blk), lambda w: (l, w, 0, 0))
    return pl.pallas_call(
        functools.partial(_lru_step_kernel, blk=blk),
        name="rglru_step",
        grid=(width // wb,),
        in_specs=[
            pl.BlockSpec((bd, wb), lambda w: (0, rxo + w)),
            pl.BlockSpec((bd, wb), lambda w: (0, rgo + w)),
            pl.BlockSpec((hist, bd, wb), lambda w: (0, 0, w)),
            pl.BlockSpec((bd, wb), lambda w: (0, w)),
            pl.BlockSpec((None, _CONV_WIDTH, wb), lambda w: (l, 0, w)),
            vec(), gate_w(), gate_w(), vec(), vec(), vec(),
        ],
        out_specs=[pl.BlockSpec((bd, wb), lambda w: (0, w)), pl.BlockSpec((bd, wb), lambda w: (0, w))],
        out_shape=[jax.ShapeDtypeStruct((bd, width), _BF16), jax.ShapeDtypeStruct((bd, width), _F32)],
        compiler_params=pltpu.CompilerParams(dimension_semantics=("parallel",)),
    )(z, z, conv_prev_t, h_prev, conv_w, conv_b3, wr, wi, br3, bi3, lam3)


def _merge(o, y, w_att, w_lru, gates, l, tok):
    m, ko = o.shape
    ky = y.shape[1]
    d = w_att.shape[-1]
    tm = tok.tm
    assert ko == ky
    tk, tn = _mm_tiles(ko, d, n_w=2)
    nb = d // tn

    def epi(accs, ex, outs):
        outs[0][...] = (ex[0][...] * accs[0] + ex[1][...] * accs[1]).astype(_BF16)

    (out,) = _matmul(
        [(o, _a_spec(tm, tk)), (y, _a_spec(tm, tk))],
        [(w_att, _w_spec((l,), tk, tn)), (w_lru, _w_spec((l,), tk, tn))],
        [0, 1],
        [(gates, _row_spec(tm, tn)), (gates, _row_spec(tm, tn, lambda n: n + nb))],
        [(jax.ShapeDtypeStruct((m, d), _BF16), _row_spec(tm, tn))],
        grid=(tok.m_tiles, nb, ko // tk), tm=tm, tn=tn, tk=tk, epilogue=epi, name="branch_merge")
    return out


def _residual_mm(a, w, lead, res, gate, tok, *, name, k0=0, kn=None, partial_in=None, partial_out=False):
    m, ka = a.shape
    d = w.shape[-1]
    kn = ka if kn is None else kn
    tm = tok.tm
    tk, tn = _mm_tiles(kn, d, k0=k0)
    koff = k0 // tk
    ex = []
    if partial_in is not None:
        ex.append((partial_in, _row_spec(tm, tn)))
    if not partial_out:
        ex.append((res, _row_spec(tm, tn)))
        ex.append(tok.mod(gate, tn))

    def epi(accs, exr, outs):
        acc = accs[0]
        i = 0
        if partial_in is not None:
            acc = acc + exr[i][...]
            i += 1
        if partial_out:
            outs[0][...] = acc
        else:
            outs[0][...] = exr[i][...] + exr[i + 1][...] * acc

    (out,) = _matmul(
        [(a, _a_spec(tm, tk, k_of=lambda k: k + koff))],
        [(w, _w_spec(lead, tk, tn, k_of=lambda k: k + koff))],
        [0], ex,
        [(jax.ShapeDtypeStruct((m, d), _F32), _row_spec(tm, tn))],
        grid=(tok.m_tiles, d // tn, kn // tk), tm=tm, tn=tn, tk=tk, epilogue=epi, name=name)
    return out


def _swiglu_up(u, w1, w3, lead, tok):
    m, kdim = u.shape
    n = w1.shape[-1]
    tm = tok.tm
    tk, tn = _mm_tiles(kdim, n, n_w=2)
    n_of = lambda j: j

    def epi(accs, ex, outs):
        g = accs[0]
        outs[0][...] = (g * jax.nn.sigmoid(g) * accs[1]).astype(_BF16)

    (out,) = _matmul(
        [(u, _a_spec(tm, tk))],
        [(w1, _w_spec(lead, tk, tn, n_of=n_of)), (w3, _w_spec(lead, tk, tn, n_of=n_of))],
        [0, 0], [],
        [(jax.ShapeDtypeStruct((m, n), _BF16), _row_spec(tm, tn))],
        grid=(tok.m_tiles, n // tn, kdim // tk), tm=tm, tn=tn, tk=tk, epilogue=epi, name="swiglu_up")
    return out


def _split_cols(n, t):
    main = (n // t) * t
    return main, n - main


def _dense_ffn(u2, w1, w3, w2, li, res, gate, tok):
    n = w1.shape[-1]
    h = _swiglu_up(u2, w1, w3, (li,), tok)
    main, tail = _split_cols(n, _TK)
    if tail == 0 or main == 0:
        return _residual_mm(h, w2, (li,), res, gate, tok, name="ffn_down")
    part = _residual_mm(h, w2, (li,), res, gate, tok, k0=0, kn=main, partial_out=True, name="ffn_down_main")
    return _residual_mm(h, w2, (li,), res, gate, tok, k0=main, kn=tail, partial_in=part, name="ffn_down_tail")


_MOE_TILE = 1536
_MOE_SUB = 128
_MOE_CHUNK = 256


def _route_rank_kernel(*refs, n_groups, tiles):
    route_refs = refs[:n_groups]
    rank_refs = refs[n_groups:2 * n_groups]
    cnt_ref, carry = refs[2 * n_groups:]
    i = pl.program_id(0)

    @pl.when(i == 0)
    def _():
        carry[...] = jnp.zeros(carry.shape, _F32)

    def tile(route_ref, rank_ref):
        r = route_ref[...]
        tr, n_e = r.shape
        lane = lax.broadcasted_iota(jnp.int32, (tr, n_e), 1).astype(_F32)
        oh1 = jnp.where(lane == r[:, 0:1], 1.0, 0.0)
        oh2 = jnp.where(lane == r[:, 1:2], 1.0, 0.0)
        oh = oh1 + oh2
        tri = jnp.where(
            lax.broadcasted_iota(jnp.int32, (tr, tr), 1) <= lax.broadcasted_iota(jnp.int32, (tr, tr), 0),
            1.0, 0.0).astype(_BF16)
        cum = jnp.dot(tri, oh.astype(_BF16), preferred_element_type=_F32) + carry[...]
        excl = cum - oh
        rank1 = jnp.sum(oh1 * excl, axis=-1, keepdims=True)
        rank2 = jnp.sum(oh2 * excl, axis=-1, keepdims=True)
        rank_ref[...] = jnp.where(lane == 0.0, rank1, jnp.where(lane == 1.0, rank2, 0.0))
        carry[...] = cum[tr - 1:tr, :]
        cnt_ref[...] = cum[tr - 1:tr, :]

    first = 0
    for g in range(n_groups):
        @pl.when(jnp.logical_and(i >= first, i < first + tiles[g]))
        def _():
            tile(route_refs[g], rank_refs[g])
        first += tiles[g]


def _route_rank(routes):
    n_e = routes[0].shape[1]
    trs = [_pick(r.shape[0], _MOE_CHUNK) for r in routes]
    tiles = [r.shape[0] // t for r, t in zip(routes, trs)]
    firsts = [sum(tiles[:g]) for g in range(len(routes))]

    def spec(g):
        return pl.BlockSpec((trs[g], n_e), lambda i: (jnp.clip(i - firsts[g], 0, tiles[g] - 1), 0))

    specs = [spec(g) for g in range(len(routes))]
    res = pl.pallas_call(
        functools.partial(_route_rank_kernel, n_groups=len(routes), tiles=tuple(tiles)),
        name="moe_route_rank",
        grid=(sum(tiles),),
        in_specs=specs,
        out_specs=specs + [pl.BlockSpec((1, n_e), lambda i: (0, 0))],
        out_shape=[jax.ShapeDtypeStruct(r.shape, _F32) for r in routes] + [jax.ShapeDtypeStruct((1, n_e), _F32)],
        scratch_shapes=[pltpu.VMEM((1, n_e), _F32)],
        compiler_params=pltpu.CompilerParams(dimension_semantics=("arbitrary",)),
    )(*routes)
    return res[:-1], res[-1]


class _MoePlan:
    def __init__(self, routes, n_e):
        tm, sub = _MOE_TILE, _MOE_SUB
        m_all = sum(r.shape[0] for r in routes)
        self.n_tiles = (_TOP_K * m_all + n_e * (sub - 1)) // tm + n_e
        self.rows = self.n_tiles * tm
        padded_routes = [jnp.pad(r, ((0, (-r.shape[0]) % _LANES), (0, 0)), constant_values=-1.0) for r in routes]
        ranks, counts = _route_rank(padded_routes)
        ranks = [rk[:r.shape[0]] for rk, r in zip(ranks, routes)]
        counts = counts[0].astype(jnp.int32)
        padded = (counts + sub - 1) // sub * sub
        nt_e = (padded + tm - 1) // tm
        per_tile = jnp.maximum((padded // jnp.maximum(nt_e, 1) + sub - 1) // sub * sub, sub)
        t_end = jnp.cumsum(nt_e)
        t_start = t_end - nt_e
        used = t_end[-1]
        self.pos = []
        for route, rank in zip(routes, ranks):
            idx = route[:, :_TOP_K].astype(jnp.int32)
            rk = rank[:, :_TOP_K].astype(jnp.int32)
            self.pos.append((t_start[idx] + rk // per_tile[idx]) * tm + rk % per_tile[idx])
        ti = jnp.arange(self.n_tiles, dtype=jnp.int32)
        te = jnp.minimum(jnp.searchsorted(t_end, ti, side="right").astype(jnp.int32), n_e - 1)
        live = ti < used
        te_last = jnp.minimum(jnp.searchsorted(t_end, used - 1, side="right").astype(jnp.int32), n_e - 1)
        self.tile_expert = jnp.where(live, te, te_last)
        self.tile_valid = jnp.where(
            live, jnp.clip(padded[te] - (ti - t_start[te]) * per_tile[te], 0, per_tile[te]), 0)
        flat = jnp.concatenate([p_.reshape(-1) for p_ in self.pos])
        tok = jnp.repeat(jnp.arange(m_all, dtype=jnp.int32), _TOP_K)
        self.row_token = jnp.zeros((self.rows,), jnp.int32).at[flat].set(tok)
        per = tm // _MOE_CHUNK
        ci = jnp.arange(self.rows // _MOE_CHUNK, dtype=jnp.int32)
        self.chunk_valid = ((ci % per) * _MOE_CHUNK < self.tile_valid[ci // per]).astype(jnp.int32)


_GATHER_UNROLL = 8


def _dispatch_kernel(tok_ref, cv_ref, src_ref, o_ref, buf, sems, *, sub):
    c = pl.program_id(0)
    n_chunks = pl.num_programs(0)

    def issue(chunk, slot):
        def body(r, carry):
            tok = tok_ref[chunk * sub + r]
            pltpu.make_async_copy(src_ref.at[pl.ds(tok, 1), :], buf.at[slot, pl.ds(r, 1), :], sems.at[slot]).start()
            return carry
        lax.fori_loop(0, sub, body, 0, unroll=_GATHER_UNROLL)

    def await_all(slot):
        pltpu.make_async_copy(src_ref.at[pl.ds(0, sub), :], buf.at[slot], sems.at[slot]).wait()

    slot = lax.rem(c, 2)

    @pl.when(jnp.logical_and(c == 0, cv_ref[0] > 0))
    def _():
        issue(0, 0)

    nxt = jnp.minimum(c + 1, n_chunks - 1)

    @pl.when(jnp.logical_and(c + 1 < n_chunks, cv_ref[nxt] > 0))
    def _():
        issue(c + 1, 1 - slot)

    @pl.when(cv_ref[c] > 0)
    def _():
        await_all(slot)
        o_ref[...] = buf[slot].astype(o_ref.dtype)

    @pl.when(cv_ref[c] == 0)
    def _():
        o_ref[...] = jnp.zeros(o_ref.shape, o_ref.dtype)


def _moe_dispatch(u_all, plan):
    m, d = u_all.shape
    sub = _MOE_CHUNK
    assert m >= sub and _MOE_TILE % sub == 0
    return pl.pallas_call(
        functools.partial(_dispatch_kernel, sub=sub),
        name="moe_dispatch",
        grid_spec=pltpu.PrefetchScalarGridSpec(
            num_scalar_prefetch=2,
            grid=(plan.rows // sub,),
            in_specs=[pl.BlockSpec(memory_space=pl.ANY)],
            out_specs=pl.BlockSpec((sub, d), lambda c, tok, cv: (c, 0)),
            scratch_shapes=[pltpu.VMEM((2, sub, d), u_all.dtype), pltpu.SemaphoreType.DMA((2,))]),
        out_shape=jax.ShapeDtypeStruct((plan.rows, d), _BF16),
        compiler_params=pltpu.CompilerParams(
            dimension_semantics=("arbitrary",),
            vmem_limit_bytes=_vmem_limit(8 * sub * d * 4)),
    )(plan.row_token, plan.chunk_valid, u_all)


def _gmm_body(te_ref, tv_ref, a_ref, *rest, n_w, n_ex, nk, tm, sub, epilogue):
    w_refs = rest[:n_w]
    ex_refs = rest[n_w:n_w + n_ex]
    o_ref = rest[n_w + n_ex]
    acc_refs = rest[n_w + n_ex + 1:]
    k = pl.program_id(2)
    valid = tv_ref[pl.program_id(0)]

    def accumulate(rows):
        def products():
            a = a_ref[rows, :]
            return [jnp.dot(a, w[...].astype(_BF16), preferred_element_type=_F32) for w in w_refs]

        if nk == 1:
            o_ref[rows, :] = epilogue(products(), ex_refs, rows).astype(o_ref.dtype)
            return

        @pl.when(k == 0)
        def _():
            for j in range(n_w):
                acc_refs[j][rows, :] = jnp.zeros((rows.size, acc_refs[j].shape[1]), _F32)

        parts = products()
        for j in range(n_w):
            acc_refs[j][rows, :] += parts[j]

        @pl.when(k == nk - 1)
        def _():
            full = [acc_refs[j][rows, :] for j in range(n_w)]
            o_ref[rows, :] = epilogue(full, ex_refs, rows).astype(o_ref.dtype)

    units = valid // sub
    start = jnp.int32(0)
    bit = 1 << ((tm // sub).bit_length() - 1)
    while bit >= 1:
        take = (units & bit) != 0
        size = bit * sub

        @pl.when(take)
        def _():
            accumulate(pl.ds(pl.multiple_of(start, sub), size))

        start = start + jnp.where(take, size, 0)
        bit //= 2

    @pl.when(k == nk - 1)
    def _():
        for s in range(tm // sub):
            @pl.when(s * sub >= valid)
            def _():
                o_ref[pl.ds(s * sub, sub), :] = jnp.zeros((sub, o_ref.shape[1]), o_ref.dtype)


def _gmm(a, w_list, mi, plan, n, ex_ops, epilogue, out_dtype, name):
    kdim = a.shape[1]
    tm, sub = _MOE_TILE, _MOE_SUB
    n_w = len(w_list)
    tk, tn = _mm_tiles(kdim, n, n_w=n_w)
    nk = kdim // tk
    live_k = lambda m, k, tv: jnp.where(tv[m] > 0, k, 0)
    a_spec = pl.BlockSpec((tm, tk), lambda m, j, k, te, tv: (m, live_k(m, k, tv)))
    w_spec = pl.BlockSpec((None, None, tk, tn),
                          lambda m, j, k, te, tv: (mi, te[m], live_k(m, k, tv), live_k(m, j, tv)))
    o_spec = pl.BlockSpec((tm, tn), lambda m, j, k, te, tv: (m, j))
    est = 2 * tm * tk * 2 + n_w * (2 * tk * tn * 4 + tk * tn * 2 + 2 * tm * tn * 4) + 4 * tm * tn * 4
    return pl.pallas_call(
        functools.partial(_gmm_body, n_w=n_w, n_ex=len(ex_ops), nk=nk, tm=tm, sub=sub, epilogue=epilogue),
        name=name,
        grid_spec=pltpu.PrefetchScalarGridSpec(
            num_scalar_prefetch=2,
            grid=(plan.n_tiles, n // tn, nk),
            in_specs=[a_spec] + [w_spec] * n_w + [s for _, s in ex_ops],
            out_specs=o_spec,
            scratch_shapes=[pltpu.VMEM((tm, tn), _F32) for _ in range(n_w)] if nk > 1 else []),
        out_shape=jax.ShapeDtypeStruct((plan.rows, n), out_dtype),
        compiler_params=pltpu.CompilerParams(
            dimension_semantics=("arbitrary", "arbitrary", "arbitrary"),
            vmem_limit_bytes=_vmem_limit(est)),
    )(plan.tile_expert, plan.tile_valid, a, *w_list, *[x for x, _ in ex_ops])


def _combine_kernel(pos_ref, y_ref, x_ref, g_ref, route_ref, o_ref, ybuf, sems, *, tc):
    i = pl.program_id(0)
    n_steps = pl.num_programs(0)

    def row_copy(step, slot, r, j):
        src = y_ref.at[pl.ds(pos_ref[_TOP_K * (step * tc + r) + j], 1), :]
        return pltpu.make_async_copy(src, ybuf.at[slot, j, pl.ds(r, 1), :], sems.at[slot])

    def issue(step, slot):
        def body(r, carry):
            for j in range(_TOP_K):
                row_copy(step, slot, r, j).start()
            return carry
        lax.fori_loop(0, tc, body, 0, unroll=min(tc, _GATHER_UNROLL))

    slot = lax.rem(i, 2)

    @pl.when(i == 0)
    def _():
        issue(0, 0)

    @pl.when(i + 1 < n_steps)
    def _():
        issue(i + 1, 1 - slot)

    for j in range(_TOP_K):
        pltpu.make_async_copy(y_ref.at[pl.ds(0, tc), :], ybuf.at[slot, j], sems.at[slot]).wait()
    route = route_ref[...]
    y = ybuf[slot, 0] * route[:, _TOP_K:_TOP_K + 1]
    for j in range(1, _TOP_K):
        y = y + ybuf[slot, j] * route[:, _TOP_K + j:_TOP_K + j + 1]
    o_ref[...] = x_ref[...] + g_ref[...] * y


def _moe_combine(y, pos, route, res, gate, tok):
    m, d = res.shape
    n_e = route.shape[1]
    tc = tok.m if tok.per_row else _pick(tok.seq, 256)
    rows = _Tokens(tok.groups, tok.seq, tc)
    g3, g_spec = rows.mod(gate, d)
    return pl.pallas_call(
        functools.partial(_combine_kernel, tc=tc),
        name="moe_combine",
        grid_spec=pltpu.PrefetchScalarGridSpec(
            num_scalar_prefetch=1,
            grid=(m // tc,),
            in_specs=[pl.BlockSpec(memory_space=pl.ANY),
                      pl.BlockSpec((tc, d), lambda i, p: (i, 0)),
                      pl.BlockSpec(g_spec.block_shape, lambda i, p, f=g_spec.index_map: f(i, 0, 0)),
                      pl.BlockSpec((tc, n_e), lambda i, p: (i, 0))],
            out_specs=pl.BlockSpec((tc, d), lambda i, p: (i, 0)),
            scratch_shapes=[pltpu.VMEM((2, _TOP_K, tc, d), _F32), pltpu.SemaphoreType.DMA((2,))]),
        out_shape=jax.ShapeDtypeStruct((m, d), _F32),
        compiler_params=pltpu.CompilerParams(
            dimension_semantics=("arbitrary",),
            vmem_limit_bytes=_vmem_limit(12 * tc * d * 4)),
    )(pos.reshape(-1), y, res, g3, route)


def _moe_ffn(u_groups, route_groups, w1, w3, w2, mi, res_groups, gate_groups, toks):
    n_e = w1.shape[1]
    n = w1.shape[-1]
    d = w2.shape[-1]
    plan = _MoePlan(route_groups, n_e)
    a_sorted = _moe_dispatch(jnp.concatenate(u_groups, axis=0), plan)

    def up_epi(accs, ex, rows):
        g = accs[0]
        return g * jax.nn.sigmoid(g) * accs[1]

    h = _gmm(a_sorted, [w1, w3], mi, plan, n, [], up_epi, _BF16, "moe_up")
    y = _gmm(h, [w2], mi, plan, d, [], lambda accs, ex, rows: accs[0], _F32, "moe_down")
    return [_moe_combine(y, pos, route, res, gate, tok)
            for pos, route, res, gate, tok in zip(plan.pos, route_groups, res_groups, gate_groups, toks)]


def kernel(x_prompt, x_sample, c_prompt, c_sample, cache_k, cache_v, state_conv, state_h, page_table, rel_bias, ada_w, ada_b, norm1_g, w_in, q_norm_g, k_norm_g, lam_q1, lam_k1, lam_q2, lam_k2, subln_g, conv_w, conv_b, lru_wr, lru_br, lru_wi, lru_bi, lru_lambda, w_att, w_lru, gate_w, gate_b, w_o, norm2_g, ffn_w1, ffn_w3, ffn_w2, router_w, router_b, moe_w1, moe_w3, moe_w2):
    depth = w_in.shape[0]
    bp, seq, d = x_prompt.shape
    bd = x_sample.shape[0]
    n_heads = cache_k.shape[3]
    dv = cache_v.shape[4]
    hd = q_norm_g.shape[-1]
    att_w = n_heads * dv
    lru_w = conv_w.shape[-1]
    page = cache_k.shape[2]
    assert x_sample.shape[1] == 1 and cache_k.shape[4] == 2 * hd == dv

    ptok = _Tokens(bp, seq, _pick(seq, 1024))
    stok = _Tokens(bd, 1, bd)
    t_att = _pick(seq, 512)

    vec3 = lambda a: a.reshape(a.shape[:-1] + (1, a.shape[-1]))
    ada_b3, norm1_g3, norm2_g3 = vec3(ada_b), vec3(norm1_g), vec3(norm2_g)
    q_norm_g3, k_norm_g3, subln_g3 = vec3(q_norm_g), vec3(k_norm_g), vec3(subln_g)
    conv_b3, lru_br3, lru_bi3, lam3 = vec3(conv_b), vec3(lru_br), vec3(lru_bi), vec3(lru_lambda)
    gate_b3, router_b3 = vec3(gate_b), vec3(router_b)
    lam_par = jnp.stack([lam_q1, lam_k1, lam_q2, lam_k2], axis=1)

    flash_bias = _bias_tiles(rel_bias, _flash_buckets(t_att))
    dec_bias = _bias_tiles(rel_bias, _decode_buckets(page))
    same_head = jnp.eye(n_heads, dtype=bool)[None, :, None, :]
    dec_bias = jnp.where(same_head, jnp.transpose(dec_bias, (1, 0, 2))[..., None], _NEG_INF)
    dec_bias = dec_bias.reshape(dec_bias.shape[0], n_heads, page * n_heads)

    c_all = jnp.concatenate([c_prompt, c_sample], axis=0)
    pad_rows = (-c_all.shape[0]) % _SUBLANES
    c_all = jnp.pad(c_all, ((0, pad_rows), (0, 0)))

    xp = x_prompt.reshape(bp * seq, d)
    xs = x_sample.reshape(bd, d)
    zeros_conv = jnp.zeros((bp, _CONV_WIDTH - 1, lru_w), x_prompt.dtype)
    zeros_h = jnp.zeros((bp, lru_w), state_h.dtype)
    sm_scale = hd ** -0.5
    rx_col, rg_col = 3 * att_w, 3 * att_w + lru_w

    outs = {k_: [] for k_ in ("kp", "vp", "cp", "hp", "ks", "vs", "cs", "hs")}
    for l in range(depth):
        lam_init = 0.8 - 0.6 * math.exp(-0.3 * l)
        mod = _ada(c_all, ada_w, ada_b3, l)
        mods_p = [mod[:bp, j * d:(j + 1) * d] for j in range(6)]
        mods_s = [mod[bp:bp + bd, j * d:(j + 1) * d] for j in range(6)]
        dense = l % 2 == 0
        li = l // 2

        def mixer(x, tok, mods):
            sh1, sc1, g1 = mods[0], mods[1], mods[2]
            u = _norm_mod(x, norm1_g3, l, sc1, sh1, tok)
            q = _proj(u, w_in, l, 0, att_w, tok, out_dtype=_BF16, norm_g3=q_norm_g3, scale=sm_scale,
                      name="proj_q")
            k = _proj(u, w_in, l, att_w, att_w, tok, out_dtype=_F32, norm_g3=k_norm_g3, name="proj_k")
            v = _proj(u, w_in, l, 2 * att_w, att_w, tok, out_dtype=_F32, name="proj_v")
            r = _proj(u, w_in, l, 3 * att_w, 2 * lru_w, tok, out_dtype=_F32, name="proj_lru")
            gates = _proj(u, gate_w, l, 0, 2 * d, tok, out_dtype=_F32, bias3=gate_b3, act="sigmoid",
                          name="proj_gates")
            return g1, q, k, v, r, gates

        def channel(x1s, toks, modss):
            if dense:
                res = []
                for x1, tok, mods in zip(x1s, toks, modss):
                    u2 = _norm_mod(x1, norm2_g3, l, mods[4], mods[3], tok)
                    res.append(_dense_ffn(u2, ffn_w1, ffn_w3, ffn_w2, li, x1, mods[5], tok))
                return res
            us, routes = [], []
            for x1, tok, mods in zip(x1s, toks, modss):
                u2, route = _norm_mod(x1, norm2_g3, l, mods[4], mods[3], tok, router=(router_w, router_b3, li))
                us.append(u2)
                routes.append(route)
            return _moe_ffn(us, routes, moe_w1, moe_w3, moe_w2, li, x1s, [m_[5] for m_ in modss], toks)

        g1, q, k, v, r, gates = mixer(xp, ptok, mods_p)
        o = _prompt_attention(q.reshape(bp, seq, att_w), k.reshape(bp, seq, att_w), v.reshape(bp, seq, att_w),
                              rel_bias, flash_bias, lam_par, subln_g3, l, lam_init, n_heads, t_att)
        r3 = r.reshape(bp, seq, 2 * lru_w)
        y, h_new = _lru_seq(r3, 0, lru_w, lru_w, zeros_conv, zeros_h, conv_w, conv_b3, lru_wr, lru_wi,
                            lru_br3, lru_bi3, lam3, l, _pick(seq, 256))
        conv_new = jnp.concatenate([zeros_conv, r3[:, -(_CONV_WIDTH - 1):, :lru_w]], axis=1)[:, -(_CONV_WIDTH - 1):]
        t_mix = _merge(o.reshape(bp * seq, att_w), y.reshape(bp * seq, lru_w), w_att, w_lru, gates, l, ptok)
        x1_p = _residual_mm(t_mix, w_o, (l,), xp, g1, ptok, name="out_proj")
        outs["kp"].append(k.reshape(bp, seq, n_heads, dv))
        outs["vp"].append(v.reshape(bp, seq, n_heads, dv))
        outs["cp"].append(conv_new)
        outs["hp"].append(h_new)

        g1, q, k, v, r, gates = mixer(xs, stok, mods_s)
        o = _sample_attention(q.reshape(bd, n_heads, dv), k.reshape(bd, n_heads, dv), v.reshape(bd, n_heads, dv),
                              cache_k, cache_v, page_table, dec_bias, lam_par, subln_g3, l, lam_init)
        conv_prev = state_conv[l]
        y, h_new = _lru_step(r, 0, lru_w, lru_w, jnp.swapaxes(conv_prev, 0, 1), state_h[l], conv_w, conv_b3,
                             lru_wr, lru_wi, lru_br3, lru_bi3, lam3, l)
        conv_new = jnp.concatenate([conv_prev, r[:, None, :lru_w]], axis=1)[:, -(_CONV_WIDTH - 1):]
        t_mix = _merge(o.reshape(bd, att_w), y, w_att, w_lru, gates, l, stok)
        x1_s = _residual_mm(t_mix, w_o, (l,), xs, g1, stok, name="out_proj")
        outs["ks"].append(k.reshape(bd, 1, n_heads, dv))
        outs["vs"].append(v.reshape(bd, 1, n_heads, dv))
        outs["cs"].append(conv_new)
        outs["hs"].append(h_new)

        xp, xs = channel([x1_p, x1_s], [ptok, stok], [mods_p, mods_s])

    st = jnp.stack
    return (xp.reshape(bp, seq, d), xs.reshape(bd, 1, d), st(outs["kp"]), st(outs["vp"]), st(outs["cp"]),
            st(outs["hp"]), st(outs["ks"]), st(outs["vs"]), st(outs["cs"]), st(outs["hs"]))
```

```python
import functools
import math

import numpy as np
import jax
import jax.numpy as jnp
from jax import lax
from jax.experimental import pallas as pl
from jax.experimental.pallas import tpu as pltpu

_F32 = jnp.float32
_BF16 = jnp.bfloat16

_EPS = 1e-6
_LRU_C = 8.0
_CONV_WIDTH = 4
_N_BUCKETS = 32
_MAX_DISTANCE = 128
_TOP_K = 2

_LANES = 128
_SUBLANES = 8
_VMEM_CAP_BYTES = 60 * 1024 * 1024
_VMEM_MIN_BYTES = 32 * 1024 * 1024

_NEG_INF = float("-inf")


def _vmem_limit(est_bytes):
    return int(min(_VMEM_CAP_BYTES, max(_VMEM_MIN_BYTES, est_bytes)))


def _nbytes(shape, dtype):
    return int(np.prod(shape)) * jnp.dtype(dtype).itemsize


def _pick(n, pref):
    t = pref
    while t >= _LANES:
        if n % t == 0:
            return t
        t //= 2
    return n


def _mm_body(*refs, n_a, w_a, n_ex, n_out, nk, a_fn, epilogue):
    n_w = len(w_a)
    a_refs = refs[:n_a]
    w_refs = refs[n_a:n_a + n_w]
    ex_refs = refs[n_a + n_w:n_a + n_w + n_ex]
    out_refs = refs[n_a + n_w + n_ex:n_a + n_w + n_ex + n_out]
    acc_refs = refs[n_a + n_w + n_ex + n_out:]
    def products():
        a_vals = [a_fn(r[...]) for r in a_refs]
        return [jnp.dot(a_vals[w_a[j]], w_refs[j][...].astype(_BF16), preferred_element_type=_F32)
                for j in range(n_w)]

    if nk == 1:
        epilogue(products(), ex_refs, out_refs)
        return
    k = pl.program_id(2)

    @pl.when(k == 0)
    def _():
        parts = products()
        for j in range(n_w):
            acc_refs[j][...] = parts[j]

    @pl.when(jnp.logical_and(k > 0, k < nk - 1))
    def _():
        parts = products()
        for j in range(n_w):
            acc_refs[j][...] += parts[j]

    @pl.when(k == nk - 1)
    def _():
        parts = products()
        epilogue([acc_refs[j][...] + parts[j] for j in range(n_w)], ex_refs, out_refs)


def _to_bf16(x):
    return x.astype(_BF16)


_TM = 1024
_FULL_K = 4096
_TN_FULL_K = 512
_TN_TILED_K = 1024
_TK = 2048


def _mm_tiles(kdim, n, n_w=1, k0=0, n0=0):
    if kdim <= _FULL_K and k0 == 0:
        tn_pref = max(2 * _LANES, min(_TN_TILED_K, _TN_FULL_K * _FULL_K // (kdim * n_w)))
        return kdim, _pick(math.gcd(n, n0), 1 << (tn_pref.bit_length() - 1))
    return _pick(math.gcd(kdim, k0), _TK), _pick(math.gcd(n, n0), _TN_TILED_K)


def _matmul(a_ops, w_ops, w_a, ex_ops, outs, *, grid, tm, tn, tk, epilogue, name, a_fn=_to_bf16):
    nk = grid[2]
    n_w = len(w_ops)
    scratch = [pltpu.VMEM((tm, tn), _F32) for _ in range(n_w)] if nk > 1 else []
    est = 0
    for arr, spec in a_ops + w_ops + ex_ops:
        blk = [d for d in spec.block_shape if d is not None]
        est += 2 * _nbytes(blk, arr.dtype)
    for sds, spec in outs:
        blk = [d for d in spec.block_shape if d is not None]
        est += 2 * _nbytes(blk, sds.dtype)
    est += n_w * (2 * tm * tn * 4 + tk * tn * 2) + 2 * tm * tn * 4
    body = functools.partial(
        _mm_body, n_a=len(a_ops), w_a=tuple(w_a), n_ex=len(ex_ops), n_out=len(outs),
        nk=nk, a_fn=a_fn, epilogue=epilogue)
    res = pl.pallas_call(
        body,
        name=name,
        grid=grid,
        in_specs=[s for _, s in a_ops + w_ops + ex_ops],
        out_specs=[s for _, s in outs],
        out_shape=[s for s, _ in outs],
        scratch_shapes=scratch,
        compiler_params=pltpu.CompilerParams(
            dimension_semantics=("parallel", "parallel", "arbitrary"),
            vmem_limit_bytes=_vmem_limit(est)),
    )(*[a for a, _ in a_ops + w_ops + ex_ops])
    return res


class _Tokens:
    def __init__(self, groups, seq, tm):
        self.groups, self.seq, self.tm = groups, seq, tm
        self.m = groups * seq
        if seq % tm == 0:
            self.per_row = False
        else:
            assert seq == 1 and tm == self.m
            self.per_row = True
        self.m_tiles = self.m // tm

    def mod(self, arr2d, tn, n_of=lambda n: n):
        g, d = arr2d.shape
        assert g == self.groups
        if self.per_row:
            a3 = arr2d.reshape(1, g, d)
            return a3, pl.BlockSpec((None, g, tn), lambda m, n, k: (0, 0, n_of(n)))
        a3 = arr2d.reshape(g, 1, d)
        per = self.seq // self.tm
        return a3, pl.BlockSpec((None, 1, tn), lambda m, n, k: (m // per, 0, n_of(n)))


def _row_spec(tm, tn, n_of=lambda n: n):
    return pl.BlockSpec((tm, tn), lambda m, n, k: (m, n_of(n)))


def _a_spec(tm, tk, k_of=lambda k: k):
    return pl.BlockSpec((tm, tk), lambda m, n, k: (m, k_of(k)))


def _w_spec(lead, tk, tn, k_of=lambda k: k, n_of=lambda n: n):
    nl = len(lead)
    return pl.BlockSpec((None,) * nl + (tk, tn), lambda m, n, k: tuple(lead) + (k_of(k), n_of(n)))


def _vec_spec(lead, tn, n_of=lambda n: n):
    nl = len(lead)
    return pl.BlockSpec((None,) * nl + (1, tn), lambda m, n, k: tuple(lead) + (0, n_of(n)))


def _ada(c_all, ada_w, ada_b3, l):
    m, d = c_all.shape
    n = ada_w.shape[-1]
    tn = _pick(n, 512)

    def a_fn(c):
        return (c * jax.nn.sigmoid(c)).astype(_BF16)

    def epi(accs, ex, outs):
        outs[0][...] = accs[0] + ex[0][...]

    (out,) = _matmul(
        [(c_all, pl.BlockSpec((m, d), lambda i, n_, k: (0, 0)))],
        [(ada_w, _w_spec((l,), d, tn))],
        [0],
        [(ada_b3, _vec_spec((l,), tn))],
        [(jax.ShapeDtypeStruct((m, n), _F32), pl.BlockSpec((m, tn), lambda i, n_, k: (0, n_)))],
        grid=(1, n // tn, 1), tm=m, tn=tn, tk=d, epilogue=epi, a_fn=a_fn, name="ada_mod")
    return out


def _norm_mod_kernel(x_ref, g_ref, sc_ref, sh_ref, *rest, router):
    x = x_ref[...]
    y = x * lax.rsqrt(jnp.mean(x * x, axis=-1, keepdims=True) + _EPS) * g_ref[...]
    u = y * (1.0 + sc_ref[...]) + sh_ref[...]
    if not router:
        (o_ref,) = rest
        o_ref[...] = u.astype(o_ref.dtype)
        return
    rw_ref, rb_ref, o_ref, gate_ref = rest
    o_ref[...] = u.astype(o_ref.dtype)
    n_e = rb_ref.shape[-1]
    logits = jnp.dot(u, rw_ref[...], preferred_element_type=_F32,
                     precision=lax.Precision.HIGHEST) + rb_ref[...]
    lane = lax.broadcasted_iota(jnp.int32, logits.shape, 1).astype(_F32)
    v1 = jnp.max(logits, axis=-1, keepdims=True)
    i1 = jnp.min(jnp.where(logits == v1, lane, float(n_e)), axis=-1, keepdims=True)
    rest_l = jnp.where(lane == i1, _NEG_INF, logits)
    v2 = jnp.max(rest_l, axis=-1, keepdims=True)
    i2 = jnp.min(jnp.where(rest_l == v2, lane, float(n_e)), axis=-1, keepdims=True)
    e2 = jnp.exp(v2 - v1)
    w1 = 1.0 / (1.0 + e2)
    w2 = e2 / (1.0 + e2)
    gate_ref[...] = jnp.where(lane == 0.0, i1, jnp.where(lane == 1.0, i2, jnp.where(
        lane == 2.0, w1, jnp.where(lane == 3.0, w2, 0.0))))


def _norm_mod(x, g3, l, sc, sh, tok, router=None):
    m, d = x.shape
    tm = tok.m if tok.per_row else _pick(tok.seq, 256)
    rows = _Tokens(tok.groups, tok.seq, tm)
    sc3, sc_spec = rows.mod(sc, d)
    sh3, sh_spec = rows.mod(sh, d)
    fix = lambda spec: pl.BlockSpec(spec.block_shape, lambda i, f=spec.index_map: f(i, 0, 0))
    in_ops = [
        (x, pl.BlockSpec((tm, d), lambda i: (i, 0))),
        (g3, pl.BlockSpec((None, 1, d), lambda i: (l, 0, 0))),
        (sc3, fix(sc_spec)),
        (sh3, fix(sh_spec)),
    ]
    u_dtype = _BF16 if router is None else _F32
    outs = [(jax.ShapeDtypeStruct((m, d), u_dtype), pl.BlockSpec((tm, d), lambda i: (i, 0)))]
    if router is not None:
        rw, rb3, mi = router
        n_e = rw.shape[-1]
        in_ops += [
            (rw, pl.BlockSpec((None, d, n_e), lambda i: (mi, 0, 0))),
            (rb3, pl.BlockSpec((None, 1, n_e), lambda i: (mi, 0, 0))),
        ]
        outs.append((jax.ShapeDtypeStruct((m, n_e), _F32), pl.BlockSpec((tm, n_e), lambda i: (i, 0))))
    res = pl.pallas_call(
        functools.partial(_norm_mod_kernel, router=router is not None),
        name="norm_mod_route" if router is not None else "norm_mod",
        grid=(m // tm,),
        in_specs=[s for _, s in in_ops],
        out_specs=[s for _, s in outs],
        out_shape=[s for s, _ in outs],
        compiler_params=pltpu.CompilerParams(
            dimension_semantics=("parallel",),
            vmem_limit_bytes=_vmem_limit(8 * tm * d * 4)),
    )(*[a for a, _ in in_ops])
    return res if router is not None else res[0]


def _proj(u, w, l, col0, n, tok, *, out_dtype, name, norm_g3=None, scale=1.0, bias3=None, act=None):
    m, kdim = u.shape
    tm = tok.tm
    tk, tn = _mm_tiles(kdim, n, n0=col0)
    off = col0 // tn
    n_of = lambda j: j + off
    ex = []
    if norm_g3 is not None:
        hd = norm_g3.shape[-1]
        ex.append((norm_g3, pl.BlockSpec((None, 1, hd), lambda i, j, k: (l, 0, 0))))
    if bias3 is not None:
        ex.append((bias3, _vec_spec((l,), tn, n_of)))

    def epi(accs, exr, outs):
        acc = accs[0]
        if bias3 is not None:
            acc = acc + exr[-1][...]
        if norm_g3 is not None:
            g = exr[0][...] * scale
            hd_ = g.shape[-1]
            for c in range(tn // hd_):
                blk = acc[:, c * hd_:(c + 1) * hd_]
                y = blk * lax.rsqrt(jnp.mean(blk * blk, axis=-1, keepdims=True) + _EPS) * g
                outs[0][:, c * hd_:(c + 1) * hd_] = y.astype(out_dtype)
            return
        if act == "sigmoid":
            acc = jax.nn.sigmoid(acc)
        outs[0][...] = acc.astype(out_dtype)

    (out,) = _matmul(
        [(u, _a_spec(tm, tk))],
        [(w, _w_spec((l,), tk, tn, n_of=n_of))],
        [0], ex,
        [(jax.ShapeDtypeStruct((m, n), out_dtype), _row_spec(tm, tn))],
        grid=(tok.m_tiles, n // tn, kdim // tk), tm=tm, tn=tn, tk=tk, epilogue=epi, name=name)
    return out


def _bucket_of_distance(n):
    n = np.maximum(n, 0)
    max_exact = _N_BUCKETS // 2
    nf = np.maximum(n, 1).astype(np.float32)
    large = max_exact + (np.log(nf / np.float32(max_exact)) / np.float32(math.log(_MAX_DISTANCE / max_exact))
                         * np.float32(_N_BUCKETS - max_exact)).astype(np.int32)
    large = np.minimum(large, _N_BUCKETS - 1)
    return np.where(n < max_exact, n, large).astype(np.int32)


def _bias_kernel(rel_ref, bkt_ref, o_ref):
    h = pl.program_id(0)
    b = bkt_ref[...]
    tile = jnp.zeros(b.shape, _F32)
    for j in range(_N_BUCKETS):
        tile = jnp.where(b == j, rel_ref[j, h], tile)
    o_ref[...] = jnp.where(b < 0, _NEG_INF, tile)


def _bias_tiles(rel_bias, buckets):
    n_h = rel_bias.shape[1]
    shp = buckets.shape
    zeros = (0,) * len(shp)
    return pl.pallas_call(
        _bias_kernel,
        name="rel_bias_tiles",
        grid=(n_h,),
        in_specs=[pl.BlockSpec(memory_space=pltpu.SMEM),
                  pl.BlockSpec(shp, lambda h: zeros)],
        out_specs=pl.BlockSpec((None,) + shp, lambda h: (h,) + zeros),
        out_shape=jax.ShapeDtypeStruct((n_h,) + shp, _F32),
    )(rel_bias, jnp.asarray(buckets))


_FAR_BUCKET = int(_bucket_of_distance(np.array(_MAX_DISTANCE)))


def _flash_buckets(t):
    assert t >= _MAX_DISTANCE
    r = np.arange(t)[:, None]
    c = np.arange(t)[None, :]
    prev = _bucket_of_distance(t + r - c)
    diag = np.where(c <= r, _bucket_of_distance(r - c), -1)
    return np.stack([prev, diag]).astype(np.int32)


def _decode_buckets(page):
    assert page >= _MAX_DISTANCE
    r = np.arange(page)
    far = np.full((page,), _FAR_BUCKET, np.int32)
    last = _bucket_of_distance(page - r)
    new = np.where(r == 0, _bucket_of_distance(np.array(0)), -1)
    return np.stack([far, last, new]).astype(np.int32)


def _lam_value(lam_ref, lam_init):
    lp = lam_ref[...]
    s1 = jnp.sum(lp[0:1] * lp[1:2], axis=-1, keepdims=True)
    s2 = jnp.sum(lp[2:3] * lp[3:4], axis=-1, keepdims=True)
    return jnp.exp(s1) - jnp.exp(s2) + lam_init


def _softmax_init(m_sc, l_sc, acc_sc):
    m_sc[...] = jnp.full(m_sc.shape, _NEG_INF, _F32)
    l_sc[...] = jnp.zeros(l_sc.shape, _F32)
    acc_sc[...] = jnp.zeros(acc_sc.shape, _F32)


def _softmax_update(q, kb, vb, bias, m_sc, l_sc, acc_sc, hd):
    for mi in range(2):
        s = lax.dot_general(q[:, mi * hd:(mi + 1) * hd], kb[:, mi * hd:(mi + 1) * hd],
                            (((1,), (1,)), ((), ())), preferred_element_type=_F32) + bias
        m_old = m_sc[mi]
        m_new = jnp.maximum(m_old, jnp.max(s, axis=-1, keepdims=True))
        alpha = jnp.exp(m_old - m_new)
        p = jnp.exp(s - m_new)
        l_sc[mi] = alpha * l_sc[mi] + jnp.sum(p, axis=-1, keepdims=True)
        acc_sc[mi] = alpha * acc_sc[mi] + jnp.dot(p.astype(_BF16), vb, preferred_element_type=_F32)
        m_sc[mi] = m_new


def _diff_finalize(lam_ref, g_ref, o_ref, l_sc, acc_sc, lam_init):
    lam = _lam_value(lam_ref, lam_init)
    o = acc_sc[0] / l_sc[0] - lam * (acc_sc[1] / l_sc[1])
    y = o * lax.rsqrt(jnp.mean(o * o, axis=-1, keepdims=True) + _EPS) * g_ref[...]
    o_ref[...] = (y * (1.0 - lam_init)).astype(o_ref.dtype)


_ATT_HEADS_PER_STEP = 2


def _flash_kernel(qi_ref, ki_ref, rel_ref, lam_ref, g_ref, q_ref, k_ref, v_ref, bias_ref, o_ref,
                  m_sc, l_sc, acc_sc, *, hd, lam_init, heads):
    hg = pl.program_id(1)
    step = pl.program_id(2)
    qi = qi_ref[step]
    ki = ki_ref[step]
    dv = 2 * hd

    @pl.when(ki == 0)
    def _():
        _softmax_init(m_sc, l_sc, acc_sc)

    def update(bias_of):
        for j in range(heads):
            cols = pl.ds(j * dv, dv)
            _softmax_update(q_ref[:, cols], k_ref[:, cols].astype(_BF16), v_ref[:, cols].astype(_BF16),
                            bias_of(j), m_sc.at[j], l_sc.at[j], acc_sc.at[j], hd)

    @pl.when(ki < qi - 1)
    def _():
        update(lambda j: rel_ref[_FAR_BUCKET, hg * heads + j])

    @pl.when(ki >= qi - 1)
    def _():
        update(lambda j: bias_ref[j])

    @pl.when(ki == qi)
    def _():
        for j in range(heads):
            _diff_finalize(lam_ref, g_ref, o_ref.at[:, pl.ds(j * dv, dv)], l_sc.at[j], acc_sc.at[j], lam_init)


def _prompt_attention(q, k, v, rel_bias, bias, lam_par, subln_g3, l, lam_init, n_heads, t):
    b, s, width = q.shape
    dv = width // n_heads
    hd = dv // 2
    nq = s // t
    pairs = [(qi, ki) for qi in range(nq) for ki in range(qi + 1)]
    qi_tab = jnp.asarray([p_[0] for p_ in pairs], jnp.int32)
    ki_tab = jnp.asarray([p_[1] for p_ in pairs], jnp.int32)
    q_map = lambda b_, h, p, qt, kt: (b_, qt[p], h)
    kv_map = lambda b_, h, p, qt, kt: (b_, kt[p], h)
    bias_map = lambda b_, h, p, qt, kt: (h, jnp.where(kt[p] == qt[p], 1, 0), 0, 0)
    par = lambda b_, h, p, qt, kt: (l, 0, 0)
    hs = _ATT_HEADS_PER_STEP if n_heads % _ATT_HEADS_PER_STEP == 0 else 1
    return pl.pallas_call(
        functools.partial(_flash_kernel, hd=hd, lam_init=lam_init, heads=hs),
        name="prompt_attention",
        grid_spec=pltpu.PrefetchScalarGridSpec(
            num_scalar_prefetch=2,
            grid=(b, n_heads // hs, len(pairs)),
            in_specs=[
                pl.BlockSpec(memory_space=pltpu.SMEM),
                pl.BlockSpec((None, 4, hd), par),
                pl.BlockSpec((None, 1, dv), par),
                pl.BlockSpec((None, t, hs * dv), q_map),
                pl.BlockSpec((None, t, hs * dv), kv_map),
                pl.BlockSpec((None, t, hs * dv), kv_map),
                pl.BlockSpec((hs, None, t, t), bias_map),
            ],
            out_specs=pl.BlockSpec((None, t, hs * dv), q_map),
            scratch_shapes=[pltpu.VMEM((hs, 2, t, 1), _F32), pltpu.VMEM((hs, 2, t, 1), _F32),
                            pltpu.VMEM((hs, 2, t, dv), _F32)]),
        out_shape=jax.ShapeDtypeStruct((b, s, width), _BF16),
        compiler_params=pltpu.CompilerParams(
            dimension_semantics=("parallel", "parallel", "arbitrary"),
            vmem_limit_bytes=_vmem_limit(24 * hs * t * t * 4)),
    )(qi_tab, ki_tab, rel_bias, lam_par, subln_g3, q, k, v, bias)


def _block_softmax(q, kb, vb, bias, hd):
    out = []
    for mi in range(2):
        s = lax.dot_general(q[:, mi * hd:(mi + 1) * hd], kb[:, mi * hd:(mi + 1) * hd],
                            (((1,), (1,)), ((), ())), preferred_element_type=_F32) + bias
        m = jnp.max(s, axis=-1, keepdims=True)
        p = jnp.exp(s - m)
        out.append((m, jnp.sum(p, axis=-1, keepdims=True),
                    jnp.dot(p.astype(_BF16), vb, preferred_element_type=_F32)))
    return out


def _softmax_merge(blocks, m_sc, l_sc, acc_sc):
    for mi in range(2):
        m_old = m_sc[mi]
        m_new = m_old
        for blk in blocks:
            m_new = jnp.maximum(m_new, blk[mi][0])
        alpha = jnp.exp(m_old - m_new)
        l_new = alpha * l_sc[mi]
        acc_new = alpha * acc_sc[mi]
        for blk in blocks:
            w = jnp.exp(blk[mi][0] - m_new)
            l_new = l_new + w * blk[mi][1]
            acc_new = acc_new + w * blk[mi][2]
        m_sc[mi] = m_new
        l_sc[mi] = l_new
        acc_sc[mi] = acc_new


def _decode_kernel(pt_ref, lam_ref, g_ref, q_ref, kn_ref, vn_ref, *rest, hd, n_steps, group, lam_init):
    kc_refs = rest[:group]
    vc_refs = rest[group:2 * group]
    bias_refs = rest[2 * group:3 * group]
    bias_new_ref, o_ref, m_sc, l_sc, acc_sc = rest[3 * group:]
    p = pl.program_id(1)

    @pl.when(p == 0)
    def _():
        _softmax_init(m_sc, l_sc, acc_sc)

    def block(k_ref, v_ref, bias):
        r, nh, width = k_ref.shape
        kb = k_ref[...].reshape(r * nh, width).astype(_BF16)
        vb = v_ref[...].reshape(r * nh, width).astype(_BF16)
        return _block_softmax(q_ref[...], kb, vb, bias, hd)

    @pl.when(p < n_steps)
    def _():
        blocks = [block(kc_refs[g], vc_refs[g], bias_refs[g][...]) for g in range(group)]
        _softmax_merge(blocks, m_sc, l_sc, acc_sc)

    @pl.when(p == n_steps)
    def _():
        n_new = kn_ref.shape[0] * kn_ref.shape[1]
        _softmax_merge([block(kn_ref, vn_ref, bias_new_ref[:, :n_new])], m_sc, l_sc, acc_sc)
        _diff_finalize(lam_ref, g_ref, o_ref, l_sc, acc_sc, lam_init)


_DECODE_PAGES_PER_STEP = 4


def _sample_attention(q, k_new, v_new, cache_k, cache_v, page_table, bias, lam_par, subln_g3, l, lam_init):
    bd, n_heads, dv = q.shape
    hd = dv // 2
    n_pages = page_table.shape[1]
    page = cache_k.shape[2]
    group = max(g for g in range(1, _DECODE_PAGES_PER_STEP + 1) if n_pages % g == 0)
    n_steps = n_pages // group
    pad = ((0, 0), (0, _SUBLANES - 1), (0, 0), (0, 0))
    kn = jnp.pad(k_new[:, None], pad)
    vn = jnp.pad(v_new[:, None], pad)
    last = n_pages - 1

    def page_of(p, g):
        return jnp.minimum(p, n_steps - 1) * group + g

    def cache_spec(g):
        return pl.BlockSpec((None, None, page, n_heads, dv),
                            lambda b_, p, pt: (l, pt[b_, page_of(p, g)], 0, 0, 0))

    def bias_spec(g):
        return pl.BlockSpec((None, n_heads, page * n_heads),
                            lambda b_, p, pt: (jnp.where(page_of(p, g) < last, 0, 1), 0, 0))

    par = lambda b_, p, pt: (l, 0, 0)
    grid_spec = pltpu.PrefetchScalarGridSpec(
        num_scalar_prefetch=1,
        grid=(bd, n_steps + 1),
        in_specs=[
            pl.BlockSpec((None, 4, hd), par),
            pl.BlockSpec((None, 1, dv), par),
            pl.BlockSpec((None, n_heads, dv), lambda b_, p, pt: (b_, 0, 0)),
            pl.BlockSpec((None, _SUBLANES, n_heads, dv), lambda b_, p, pt: (b_, 0, 0, 0)),
            pl.BlockSpec((None, _SUBLANES, n_heads, dv), lambda b_, p, pt: (b_, 0, 0, 0)),
        ] + [cache_spec(g) for g in range(group)] * 2 + [bias_spec(g) for g in range(group)] + [
            pl.BlockSpec((None, n_heads, page * n_heads), lambda b_, p, pt: (2, 0, 0)),
        ],
        out_specs=pl.BlockSpec((None, n_heads, dv), lambda b_, p, pt: (b_, 0, 0)),
        scratch_shapes=[pltpu.VMEM((2, n_heads, 1), _F32), pltpu.VMEM((2, n_heads, 1), _F32),
                        pltpu.VMEM((2, n_heads, dv), _F32)],
    )
    return pl.pallas_call(
        functools.partial(_decode_kernel, hd=hd, n_steps=n_steps, group=group, lam_init=lam_init),
        name="sample_attention",
        grid_spec=grid_spec,
        out_shape=jax.ShapeDtypeStruct((bd, n_heads, dv), _BF16),
        compiler_params=pltpu.CompilerParams(
            dimension_semantics=("parallel", "arbitrary"),
            vmem_limit_bytes=_vmem_limit(8 * group * page * n_heads * dv * 4)),
    )(page_table, lam_par, subln_g3, q, kn, vn, *([cache_k] * group), *([cache_v] * group),
      *([bias] * (group + 1)))


def _gelu_tanh(x):
    return 0.5 * x * (1.0 + jnp.tanh(math.sqrt(2.0 / math.pi) * (x + 0.044715 * (x * x * x))))


def _softplus(x):
    return jnp.maximum(x, 0.0) + jnp.log1p(jnp.exp(-jnp.abs(x)))


def _lru_gates(xc, wr_ref, wi_ref, br_ref, bi_ref, lam_ref, blk):
    xcb = xc.astype(_BF16)
    r_parts, i_parts = [], []
    for j in range(xc.shape[1] // blk):
        xj = xcb[:, j * blk:(j + 1) * blk]
        r_parts.append(jnp.dot(xj, wr_ref[j].astype(_BF16), preferred_element_type=_F32))
        i_parts.append(jnp.dot(xj, wi_ref[j].astype(_BF16), preferred_element_type=_F32))
    r = jax.nn.sigmoid(jnp.concatenate(r_parts, axis=-1) + br_ref[...])
    i = jax.nn.sigmoid(jnp.concatenate(i_parts, axis=-1) + bi_ref[...])
    log_a = -_LRU_C * r * _softplus(-lam_ref[...])
    a = jnp.exp(log_a)
    b = jnp.sqrt(-jnp.tanh(log_a) * (a * a + 1.0)) * (i * xc)
    return a, b


def _lru_seq_kernel(rx_ref, rg_ref, cp_ref, h0_ref, cw_ref, cb_ref, wr_ref, wi_ref, br_ref, bi_ref, lam_ref,
                    y_ref, hl_ref, xbuf, a_sc, b_sc, h_sc, *, tt, blk):
    t = pl.program_id(2)
    nt = pl.num_programs(2)
    hist = _CONV_WIDTH - 1
    base = _SUBLANES

    @pl.when(t == 0)
    def _():
        xbuf[pl.ds(base - hist, hist), :] = cp_ref[...]
        h_sc[...] = h0_ref[...]

    @pl.when(t > 0)
    def _():
        xbuf[pl.ds(base - hist, hist), :] = xbuf[pl.ds(base + tt - hist, hist), :]

    xbuf[pl.ds(base, tt), :] = rx_ref[...]
    xc = cb_ref[...] + sum(xbuf[pl.ds(base - hist + j, tt), :] * cw_ref[pl.ds(j, 1), :]
                           for j in range(_CONV_WIDTH))
    a, b = _lru_gates(xc, wr_ref, wi_ref, br_ref, bi_ref, lam_ref, blk)
    a_sc[...] = a
    b_sc[...] = b
    row = lax.broadcasted_iota(jnp.int32, (_SUBLANES, a.shape[1]), 0)

    def group(gi, h):
        r0 = pl.multiple_of(gi * _SUBLANES, _SUBLANES)
        ag = a_sc[pl.ds(r0, _SUBLANES), :]
        bg = b_sc[pl.ds(r0, _SUBLANES), :]
        for sft in (1, 2, 4):
            a_prev = pltpu.roll(ag, sft, axis=0)
            b_prev = pltpu.roll(bg, sft, axis=0)
            keep = row >= sft
            bg = jnp.where(keep, ag * b_prev + bg, bg)
            ag = jnp.where(keep, ag * a_prev, ag)
        hg = ag * h + bg
        b_sc[pl.ds(r0, _SUBLANES), :] = hg
        return jnp.broadcast_to(hg[_SUBLANES - 1:_SUBLANES, :], hg.shape)

    h_in = jnp.broadcast_to(h_sc[...], (_SUBLANES, a.shape[1]))
    h_out = lax.fori_loop(0, tt // _SUBLANES, group, h_in)
    h_sc[...] = h_out[0:1, :]
    y_ref[...] = (b_sc[...] * _gelu_tanh(rg_ref[...])).astype(y_ref.dtype)

    @pl.when(t == nt - 1)
    def _():
        hl_ref[...] = h_out[0:1, :]


def _lru_seq(z, rx_col, rg_col, width, conv_prev, h_prev, conv_w, conv_b3, wr, wi, br3, bi3, lam3, l, tt):
    b, s, _ = z.shape
    wb = _pick(width, 512)
    blk = wr.shape[-1]
    nb = wb // blk
    hist = _CONV_WIDTH - 1
    assert rx_col % wb == 0 and rg_col % wb == 0
    rxo, rgo = rx_col // wb, rg_col // wb
    vec = lambda: pl.BlockSpec((None, 1, wb), lambda b_, w, t: (l, 0, w))
    gate_w = lambda: pl.BlockSpec((None, nb, blk, blk), lambda b_, w, t: (l, w, 0, 0))
    y, h_last = pl.pallas_call(
        functools.partial(_lru_seq_kernel, tt=tt, blk=blk),
        name="rglru_seq",
        grid=(b, width // wb, s // tt),
        in_specs=[
            pl.BlockSpec((None, tt, wb), lambda b_, w, t: (b_, t, rxo + w)),
            pl.BlockSpec((None, tt, wb), lambda b_, w, t: (b_, t, rgo + w)),
            pl.BlockSpec((None, hist, wb), lambda b_, w, t: (b_, 0, w)),
            pl.BlockSpec((None, 1, wb), lambda b_, w, t: (b_, 0, w)),
            pl.BlockSpec((None, _CONV_WIDTH, wb), lambda b_, w, t: (l, 0, w)),
            vec(), gate_w(), gate_w(), vec(), vec(), vec(),
        ],
        out_specs=[
            pl.BlockSpec((None, tt, wb), lambda b_, w, t: (b_, t, w)),
            pl.BlockSpec((None, 1, wb), lambda b_, w, t: (b_, 0, w)),
        ],
        out_shape=[jax.ShapeDtypeStruct((b, s, width), _BF16),
                   jax.ShapeDtypeStruct((b, 1, width), _F32)],
        scratch_shapes=[pltpu.VMEM((tt + _SUBLANES, wb), _F32), pltpu.VMEM((tt, wb), _F32),
                        pltpu.VMEM((tt, wb), _F32), pltpu.VMEM((1, wb), _F32)],
        compiler_params=pltpu.CompilerParams(
            dimension_semantics=("parallel", "parallel", "arbitrary"),
            vmem_limit_bytes=_vmem_limit(24 * tt * wb * 4)),
    )(z, z, conv_prev, h_prev[:, None, :], conv_w, conv_b3, wr, wi, br3, bi3, lam3)
    return y, h_last[:, 0, :]


def _lru_step_kernel(rx_ref, rg_ref, cp_ref, h0_ref, cw_ref, cb_ref, wr_ref, wi_ref, br_ref, bi_ref, lam_ref,
                     y_ref, h_ref, *, blk):
    hist = _CONV_WIDTH - 1
    xc = cb_ref[...] + rx_ref[...] * cw_ref[pl.ds(hist, 1), :]
    for j in range(hist):
        xc = xc + cp_ref[j] * cw_ref[pl.ds(j, 1), :]
    a, b = _lru_gates(xc, wr_ref, wi_ref, br_ref, bi_ref, lam_ref, blk)
    h = a * h0_ref[...] + b
    h_ref[...] = h
    y_ref[...] = (h * _gelu_tanh(rg_ref[...])).astype(y_ref.dtype)


def _lru_step(z, rx_col, rg_col, width, conv_prev_t, h_prev, conv_w, conv_b3, wr, wi, br3, bi3, lam3, l):
    bd = z.shape[0]
    wb = _pick(width, 512)
    blk = wr.shape[-1]
    nb = wb // blk
    hist = _CONV_WIDTH - 1
    rxo, rgo = rx_col // wb, rg_col // wb
    vec = lambda: pl.BlockSpec((None, 1, wb), lambda w: (l, 0, w))
    gate_w = lambda: pl.BlockSpec((None, nb, blk, blk), lambda w: (l, w, 0, 0))
    return pl.pallas_call(
        functools.partial(_lru_step_kernel, blk=blk),
        name="rglru_step",
        grid=(width // wb,),
        in_specs=[
            pl.BlockSpec((bd, wb), lambda w: (0, rxo + w)),
            pl.BlockSpec((bd, wb), lambda w: (0, rgo + w)),
            pl.BlockSpec((hist, bd, wb), lambda w: (0, 0, w)),
            pl.BlockSpec((bd, wb), lambda w: (0, w)),
            pl.BlockSpec((None, _CONV_WIDTH, wb), lambda w: (l, 0, w)),
            vec(), gate_w(), gate_w(), vec(), vec(), vec(),
        ],
        out_specs=[pl.BlockSpec((bd, wb), lambda w: (0, w)), pl.BlockSpec((bd, wb), lambda w: (0, w))],
        out_shape=[jax.ShapeDtypeStruct((bd, width), _BF16), jax.ShapeDtypeStruct((bd, width), _F32)],
        compiler_params=pltpu.CompilerParams(dimension_semantics=("parallel",)),
    )(z, z, conv_prev_t, h_prev, conv_w, conv_b3, wr, wi, br3, bi3, lam3)


def _merge(o, y, w_att, w_lru, gates, l, tok):
    m, ko = o.shape
    ky = y.shape[1]
    d = w_att.shape[-1]
    tm = tok.tm
    assert ko == ky
    tk, tn = _mm_tiles(ko, d, n_w=2)
    nb = d // tn

    def epi(accs, ex, outs):
        outs[0][...] = (ex[0][...] * accs[0] + ex[1][...] * accs[1]).astype(_BF16)

    (out,) = _matmul(
        [(o, _a_spec(tm, tk)), (y, _a_spec(tm, tk))],
        [(w_att, _w_spec((l,), tk, tn)), (w_lru, _w_spec((l,), tk, tn))],
        [0, 1],
        [(gates, _row_spec(tm, tn)), (gates, _row_spec(tm, tn, lambda n: n + nb))],
        [(jax.ShapeDtypeStruct((m, d), _BF16), _row_spec(tm, tn))],
        grid=(tok.m_tiles, nb, ko // tk), tm=tm, tn=tn, tk=tk, epilogue=epi, name="branch_merge")
    return out


def _residual_mm(a, w, lead, res, gate, tok, *, name, k0=0, kn=None, partial_in=None, partial_out=False):
    m, ka = a.shape
    d = w.shape[-1]
    kn = ka if kn is None else kn
    tm = tok.tm
    tk, tn = _mm_tiles(kn, d, k0=k0)
    koff = k0 // tk
    ex = []
    if partial_in is not None:
        ex.append((partial_in, _row_spec(tm, tn)))
    if not partial_out:
        ex.append((res, _row_spec(tm, tn)))
        ex.append(tok.mod(gate, tn))

    def epi(accs, exr, outs):
        acc = accs[0]
        i = 0
        if partial_in is not None:
            acc = acc + exr[i][...]
            i += 1
        if partial_out:
            outs[0][...] = acc
        else:
            outs[0][...] = exr[i][...] + exr[i + 1][...] * acc

    (out,) = _matmul(
        [(a, _a_spec(tm, tk, k_of=lambda k: k + koff))],
        [(w, _w_spec(lead, tk, tn, k_of=lambda k: k + koff))],
        [0], ex,
        [(jax.ShapeDtypeStruct((m, d), _F32), _row_spec(tm, tn))],
        grid=(tok.m_tiles, d // tn, kn // tk), tm=tm, tn=tn, tk=tk, epilogue=epi, name=name)
    return out


def _swiglu_up(u, w1, w3, lead, tok):
    m, kdim = u.shape
    n = w1.shape[-1]
    tm = tok.tm
    tk, tn = _mm_tiles(kdim, n, n_w=2)
    n_of = lambda j: j

    def epi(accs, ex, outs):
        g = accs[0]
        outs[0][...] = (g * jax.nn.sigmoid(g) * accs[1]).astype(_BF16)

    (out,) = _matmul(
        [(u, _a_spec(tm, tk))],
        [(w1, _w_spec(lead, tk, tn, n_of=n_of)), (w3, _w_spec(lead, tk, tn, n_of=n_of))],
        [0, 0], [],
        [(jax.ShapeDtypeStruct((m, n), _BF16), _row_spec(tm, tn))],
        grid=(tok.m_tiles, n // tn, kdim // tk), tm=tm, tn=tn, tk=tk, epilogue=epi, name="swiglu_up")
    return out


def _split_cols(n, t):
    main = (n // t) * t
    return main, n - main


def _dense_ffn(u2, w1, w3, w2, li, res, gate, tok):
    n = w1.shape[-1]
    h = _swiglu_up(u2, w1, w3, (li,), tok)
    main, tail = _split_cols(n, _TK)
    if tail == 0 or main == 0:
        return _residual_mm(h, w2, (li,), res, gate, tok, name="ffn_down")
    part = _residual_mm(h, w2, (li,), res, gate, tok, k0=0, kn=main, partial_out=True, name="ffn_down_main")
    return _residual_mm(h, w2, (li,), res, gate, tok, k0=main, kn=tail, partial_in=part, name="ffn_down_tail")


_MOE_TILE = 1536
_MOE_SUB = 128
_MOE_CHUNK = 256


def _route_rank_kernel(*refs, n_groups, tiles):
    route_refs = refs[:n_groups]
    rank_refs = refs[n_groups:2 * n_groups]
    cnt_ref, carry = refs[2 * n_groups:]
    i = pl.program_id(0)

    @pl.when(i == 0)
    def _():
        carry[...] = jnp.zeros(carry.shape, _F32)

    def tile(route_ref, rank_ref):
        r = route_ref[...]
        tr, n_e = r.shape
        lane = lax.broadcasted_iota(jnp.int32, (tr, n_e), 1).astype(_F32)
        oh1 = jnp.where(lane == r[:, 0:1], 1.0, 0.0)
        oh2 = jnp.where(lane == r[:, 1:2], 1.0, 0.0)
        oh = oh1 + oh2
        tri = jnp.where(
            lax.broadcasted_iota(jnp.int32, (tr, tr), 1) <= lax.broadcasted_iota(jnp.int32, (tr, tr), 0),
            1.0, 0.0).astype(_BF16)
        cum = jnp.dot(tri, oh.astype(_BF16), preferred_element_type=_F32) + carry[...]
        excl = cum - oh
        rank1 = jnp.sum(oh1 * excl, axis=-1, keepdims=True)
        rank2 = jnp.sum(oh2 * excl, axis=-1, keepdims=True)
        rank_ref[...] = jnp.where(lane == 0.0, rank1, jnp.where(lane == 1.0, rank2, 0.0))
        carry[...] = cum[tr - 1:tr, :]
        cnt_ref[...] = cum[tr - 1:tr, :]

    first = 0
    for g in range(n_groups):
        @pl.when(jnp.logical_and(i >= first, i < first + tiles[g]))
        def _():
            tile(route_refs[g], rank_refs[g])
        first += tiles[g]


def _route_rank(routes):
    n_e = routes[0].shape[1]
    trs = [_pick(r.shape[0], _MOE_CHUNK) for r in routes]
    tiles = [r.shape[0] // t for r, t in zip(routes, trs)]
    firsts = [sum(tiles[:g]) for g in range(len(routes))]

    def spec(g):
        return pl.BlockSpec((trs[g], n_e), lambda i: (jnp.clip(i - firsts[g], 0, tiles[g] - 1), 0))

    specs = [spec(g) for g in range(len(routes))]
    res = pl.pallas_call(
        functools.partial(_route_rank_kernel, n_groups=len(routes), tiles=tuple(tiles)),
        name="moe_route_rank",
        grid=(sum(tiles),),
        in_specs=specs,
        out_specs=specs + [pl.BlockSpec((1, n_e), lambda i: (0, 0))],
        out_shape=[jax.ShapeDtypeStruct(r.shape, _F32) for r in routes] + [jax.ShapeDtypeStruct((1, n_e), _F32)],
        scratch_shapes=[pltpu.VMEM((1, n_e), _F32)],
        compiler_params=pltpu.CompilerParams(dimension_semantics=("arbitrary",)),
    )(*routes)
    return res[:-1], res[-1]


class _MoePlan:
    def __init__(self, routes, n_e):
        tm, sub = _MOE_TILE, _MOE_SUB
        m_all = sum(r.shape[0] for r in routes)
        self.n_tiles = (_TOP_K * m_all + n_e * (sub - 1)) // tm + n_e
        self.rows = self.n_tiles * tm
        padded_routes = [jnp.pad(r, ((0, (-r.shape[0]) % _LANES), (0, 0)), constant_values=-1.0) for r in routes]
        ranks, counts = _route_rank(padded_routes)
        ranks = [rk[:r.shape[0]] for rk, r in zip(ranks, routes)]
        counts = counts[0].astype(jnp.int32)
        padded = (counts + sub - 1) // sub * sub
        nt_e = (padded + tm - 1) // tm
        per_tile = jnp.maximum((padded // jnp.maximum(nt_e, 1) + sub - 1) // sub * sub, sub)
        t_end = jnp.cumsum(nt_e)
        t_start = t_end - nt_e
        used = t_end[-1]
        self.pos = []
        for route, rank in zip(routes, ranks):
            idx = route[:, :_TOP_K].astype(jnp.int32)
            rk = rank[:, :_TOP_K].astype(jnp.int32)
            self.pos.append((t_start[idx] + rk // per_tile[idx]) * tm + rk % per_tile[idx])
        ti = jnp.arange(self.n_tiles, dtype=jnp.int32)
        te = jnp.minimum(jnp.searchsorted(t_end, ti, side="right").astype(jnp.int32), n_e - 1)
        live = ti < used
        te_last = jnp.minimum(jnp.searchsorted(t_end, used - 1, side="right").astype(jnp.int32), n_e - 1)
        self.tile_expert = jnp.where(live, te, te_last)
        self.tile_valid = jnp.where(
            live, jnp.clip(padded[te] - (ti - t_start[te]) * per_tile[te], 0, per_tile[te]), 0)
        flat = jnp.concatenate([p_.reshape(-1) for p_ in self.pos])
        tok = jnp.repeat(jnp.arange(m_all, dtype=jnp.int32), _TOP_K)
        self.row_token = jnp.zeros((self.rows,), jnp.int32).at[flat].set(tok)
        per = tm // _MOE_CHUNK
        ci = jnp.arange(self.rows // _MOE_CHUNK, dtype=jnp.int32)
        self.chunk_valid = ((ci % per) * _MOE_CHUNK < self.tile_valid[ci // per]).astype(jnp.int32)


_GATHER_UNROLL = 8


def _dispatch_kernel(tok_ref, cv_ref, src_ref, o_ref, buf, sems, *, sub):
    c = pl.program_id(0)
    n_chunks = pl.num_programs(0)

    def issue(chunk, slot):
        def body(r, carry):
            tok = tok_ref[chunk * sub + r]
            pltpu.make_async_copy(src_ref.at[pl.ds(tok, 1), :], buf.at[slot, pl.ds(r, 1), :], sems.at[slot]).start()
            return carry
        lax.fori_loop(0, sub, body, 0, unroll=_GATHER_UNROLL)

    def await_all(slot):
        pltpu.make_async_copy(src_ref.at[pl.ds(0, sub), :], buf.at[slot], sems.at[slot]).wait()

    slot = lax.rem(c, 2)

    @pl.when(jnp.logical_and(c == 0, cv_ref[0] > 0))
    def _():
        issue(0, 0)

    nxt = jnp.minimum(c + 1, n_chunks - 1)

    @pl.when(jnp.logical_and(c + 1 < n_chunks, cv_ref[nxt] > 0))
    def _():
        issue(c + 1, 1 - slot)

    @pl.when(cv_ref[c] > 0)
    def _():
        await_all(slot)
        o_ref[...] = buf[slot].astype(o_ref.dtype)

    @pl.when(cv_ref[c] == 0)
    def _():
        o_ref[...] = jnp.zeros(o_ref.shape, o_ref.dtype)


def _moe_dispatch(u_all, plan):
    m, d = u_all.shape
    sub = _MOE_CHUNK
    assert m >= sub and _MOE_TILE % sub == 0
    return pl.pallas_call(
        functools.partial(_dispatch_kernel, sub=sub),
        name="moe_dispatch",
        grid_spec=pltpu.PrefetchScalarGridSpec(
            num_scalar_prefetch=2,
            grid=(plan.rows // sub,),
            in_specs=[pl.BlockSpec(memory_space=pl.ANY)],
            out_specs=pl.BlockSpec((sub, d), lambda c, tok, cv: (c, 0)),
            scratch_shapes=[pltpu.VMEM((2, sub, d), u_all.dtype), pltpu.SemaphoreType.DMA((2,))]),
        out_shape=jax.ShapeDtypeStruct((plan.rows, d), _BF16),
        compiler_params=pltpu.CompilerParams(
            dimension_semantics=("arbitrary",),
            vmem_limit_bytes=_vmem_limit(8 * sub * d * 4)),
    )(plan.row_token, plan.chunk_valid, u_all)


def _gmm_body(te_ref, tv_ref, a_ref, *rest, n_w, n_ex, nk, tm, sub, epilogue):
    w_refs = rest[:n_w]
    ex_refs = rest[n_w:n_w + n_ex]
    o_ref = rest[n_w + n_ex]
    acc_refs = rest[n_w + n_ex + 1:]
    k = pl.program_id(2)
    valid = tv_ref[pl.program_id(0)]

    def accumulate(rows):
        def products():
            a = a_ref[rows, :]
            return [jnp.dot(a, w[...].astype(_BF16), preferred_element_type=_F32) for w in w_refs]

        if nk == 1:
            o_ref[rows, :] = epilogue(products(), ex_refs, rows).astype(o_ref.dtype)
            return

        @pl.when(k == 0)
        def _():
            for j in range(n_w):
                acc_refs[j][rows, :] = jnp.zeros((rows.size, acc_refs[j].shape[1]), _F32)

        parts = products()
        for j in range(n_w):
            acc_refs[j][rows, :] += parts[j]

        @pl.when(k == nk - 1)
        def _():
            full = [acc_refs[j][rows, :] for j in range(n_w)]
            o_ref[rows, :] = epilogue(full, ex_refs, rows).astype(o_ref.dtype)

    units = valid // sub
    start = jnp.int32(0)
    bit = 1 << ((tm // sub).bit_length() - 1)
    while bit >= 1:
        take = (units & bit) != 0
        size = bit * sub

        @pl.when(take)
        def _():
            accumulate(pl.ds(pl.multiple_of(start, sub), size))

        start = start + jnp.where(take, size, 0)
        bit //= 2

    @pl.when(k == nk - 1)
    def _():
        for s in range(tm // sub):
            @pl.when(s * sub >= valid)
            def _():
                o_ref[pl.ds(s * sub, sub), :] = jnp.zeros((sub, o_ref.shape[1]), o_ref.dtype)


def _gmm(a, w_list, mi, plan, n, ex_ops, epilogue, out_dtype, name):
    kdim = a.shape[1]
    tm, sub = _MOE_TILE, _MOE_SUB
    n_w = len(w_list)
    tk, tn = _mm_tiles(kdim, n, n_w=n_w)
    nk = kdim // tk
    live_k = lambda m, k, tv: jnp.where(tv[m] > 0, k, 0)
    a_spec = pl.BlockSpec((tm, tk), lambda m, j, k, te, tv: (m, live_k(m, k, tv)))
    w_spec = pl.BlockSpec((None, None, tk, tn),
                          lambda m, j, k, te, tv: (mi, te[m], live_k(m, k, tv), live_k(m, j, tv)))
    o_spec = pl.BlockSpec((tm, tn), lambda m, j, k, te, tv: (m, j))
    est = 2 * tm * tk * 2 + n_w * (2 * tk * tn * 4 + tk * tn * 2 + 2 * tm * tn * 4) + 4 * tm * tn * 4
    return pl.pallas_call(
        functools.partial(_gmm_body, n_w=n_w, n_ex=len(ex_ops), nk=nk, tm=tm, sub=sub, epilogue=epilogue),
        name=name,
        grid_spec=pltpu.PrefetchScalarGridSpec(
            num_scalar_prefetch=2,
            grid=(plan.n_tiles, n // tn, nk),
            in_specs=[a_spec] + [w_spec] * n_w + [s for _, s in ex_ops],
            out_specs=o_spec,
            scratch_shapes=[pltpu.VMEM((tm, tn), _F32) for _ in range(n_w)] if nk > 1 else []),
        out_shape=jax.ShapeDtypeStruct((plan.rows, n), out_dtype),
        compiler_params=pltpu.CompilerParams(
            dimension_semantics=("arbitrary", "arbitrary", "arbitrary"),
            vmem_limit_bytes=_vmem_limit(est)),
    )(plan.tile_expert, plan.tile_valid, a, *w_list, *[x for x, _ in ex_ops])


def _combine_kernel(pos_ref, y_ref, x_ref, g_ref, route_ref, o_ref, ybuf, sems, *, tc):
    i = pl.program_id(0)
    n_steps = pl.num_programs(0)

    def row_copy(step, slot, r, j):
        src = y_ref.at[pl.ds(pos_ref[_TOP_K * (step * tc + r) + j], 1), :]
        return pltpu.make_async_copy(src, ybuf.at[slot, j, pl.ds(r, 1), :], sems.at[slot])

    def issue(step, slot):
        def body(r, carry):
            for j in range(_TOP_K):
                row_copy(step, slot, r, j).start()
            return carry
        lax.fori_loop(0, tc, body, 0, unroll=min(tc, _GATHER_UNROLL))

    slot = lax.rem(i, 2)

    @pl.when(i == 0)
    def _():
        issue(0, 0)

    @pl.when(i + 1 < n_steps)
    def _():
        issue(i + 1, 1 - slot)

    for j in range(_TOP_K):
        pltpu.make_async_copy(y_ref.at[pl.ds(0, tc), :], ybuf.at[slot, j], sems.at[slot]).wait()
    route = route_ref[...]
    y = ybuf[slot, 0] * route[:, _TOP_K:_TOP_K + 1]
    for j in range(1, _TOP_K):
        y = y + ybuf[slot, j] * route[:, _TOP_K + j:_TOP_K + j + 1]
    o_ref[...] = x_ref[...] + g_ref[...] * y


def _moe_combine(y, pos, route, res, gate, tok):
    m, d = res.shape
    n_e = route.shape[1]
    tc = tok.m if tok.per_row else _pick(tok.seq, 256)
    rows = _Tokens(tok.groups, tok.seq, tc)
    g3, g_spec = rows.mod(gate, d)
    return pl.pallas_call(
        functools.partial(_combine_kernel, tc=tc),
        name="moe_combine",
        grid_spec=pltpu.PrefetchScalarGridSpec(
            num_scalar_prefetch=1,
            grid=(m // tc,),
            in_specs=[pl.BlockSpec(memory_space=pl.ANY),
                      pl.BlockSpec((tc, d), lambda i, p: (i, 0)),
                      pl.BlockSpec(g_spec.block_shape, lambda i, p, f=g_spec.index_map: f(i, 0, 0)),
                      pl.BlockSpec((tc, n_e), lambda i, p: (i, 0))],
            out_specs=pl.BlockSpec((tc, d), lambda i, p: (i, 0)),
            scratch_shapes=[pltpu.VMEM((2, _TOP_K, tc, d), _F32), pltpu.SemaphoreType.DMA((2,))]),
        out_shape=jax.ShapeDtypeStruct((m, d), _F32),
        compiler_params=pltpu.CompilerParams(
            dimension_semantics=("arbitrary",),
            vmem_limit_bytes=_vmem_limit(12 * tc * d * 4)),
    )(pos.reshape(-1), y, res, g3, route)


def _moe_ffn(u_groups, route_groups, w1, w3, w2, mi, res_groups, gate_groups, toks):
    n_e = w1.shape[1]
    n = w1.shape[-1]
    d = w2.shape[-1]
    plan = _MoePlan(route_groups, n_e)
    a_sorted = _moe_dispatch(jnp.concatenate(u_groups, axis=0), plan)

    def up_epi(accs, ex, rows):
        g = accs[0]
        return g * jax.nn.sigmoid(g) * accs[1]

    h = _gmm(a_sorted, [w1, w3], mi, plan, n, [], up_epi, _BF16, "moe_up")
    y = _gmm(h, [w2], mi, plan, d, [], lambda accs, ex, rows: accs[0], _F32, "moe_down")
    return [_moe_combine(y, pos, route, res, gate, tok)
            for pos, route, res, gate, tok in zip(plan.pos, route_groups, res_groups, gate_groups, toks)]


def kernel(x_prompt, x_sample, c_prompt, c_sample, cache_k, cache_v, state_conv, state_h, page_table, rel_bias, ada_w, ada_b, norm1_g, w_in, q_norm_g, k_norm_g, lam_q1, lam_k1, lam_q2, lam_k2, subln_g, conv_w, conv_b, lru_wr, lru_br, lru_wi, lru_bi, lru_lambda, w_att, w_lru, gate_w, gate_b, w_o, norm2_g, ffn_w1, ffn_w3, ffn_w2, router_w, router_b, moe_w1, moe_w3, moe_w2):
    depth = w_in.shape[0]
    bp, seq, d = x_prompt.shape
    bd = x_sample.shape[0]
    n_heads = cache_k.shape[3]
    dv = cache_v.shape[4]
    hd = q_norm_g.shape[-1]
    att_w = n_heads * dv
    lru_w = conv_w.shape[-1]
    page = cache_k.shape[2]
    assert x_sample.shape[1] == 1 and cache_k.shape[4] == 2 * hd == dv

    ptok = _Tokens(bp, seq, _pick(seq, 1024))
    stok = _Tokens(bd, 1, bd)
    t_att = _pick(seq, 512)

    vec3 = lambda a: a.reshape(a.shape[:-1] + (1, a.shape[-1]))
    ada_b3, norm1_g3, norm2_g3 = vec3(ada_b), vec3(norm1_g), vec3(norm2_g)
    q_norm_g3, k_norm_g3, subln_g3 = vec3(q_norm_g), vec3(k_norm_g), vec3(subln_g)
    conv_b3, lru_br3, lru_bi3, lam3 = vec3(conv_b), vec3(lru_br), vec3(lru_bi), vec3(lru_lambda)
    gate_b3, router_b3 = vec3(gate_b), vec3(router_b)
    lam_par = jnp.stack([lam_q1, lam_k1, lam_q2, lam_k2], axis=1)

    flash_bias = _bias_tiles(rel_bias, _flash_buckets(t_att))
    dec_bias = _bias_tiles(rel_bias, _decode_buckets(page))
    same_head = jnp.eye(n_heads, dtype=bool)[None, :, None, :]
    dec_bias = jnp.where(same_head, jnp.transpose(dec_bias, (1, 0, 2))[..., None], _NEG_INF)
    dec_bias = dec_bias.reshape(dec_bias.shape[0], n_heads, page * n_heads)

    c_all = jnp.concatenate([c_prompt, c_sample], axis=0)
    pad_rows = (-c_all.shape[0]) % _SUBLANES
    c_all = jnp.pad(c_all, ((0, pad_rows), (0, 0)))

    xp = x_prompt.reshape(bp * seq, d)
    xs = x_sample.reshape(bd, d)
    zeros_conv = jnp.zeros((bp, _CONV_WIDTH - 1, lru_w), x_prompt.dtype)
    zeros_h = jnp.zeros((bp, lru_w), state_h.dtype)
    sm_scale = hd ** -0.5
    rx_col, rg_col = 3 * att_w, 3 * att_w + lru_w

    outs = {k_: [] for k_ in ("kp", "vp", "cp", "hp", "ks", "vs", "cs", "hs")}
    for l in range(depth):
        lam_init = 0.8 - 0.6 * math.exp(-0.3 * l)
        mod = _ada(c_all, ada_w, ada_b3, l)
        mods_p = [mod[:bp, j * d:(j + 1) * d] for j in range(6)]
        mods_s = [mod[bp:bp + bd, j * d:(j + 1) * d] for j in range(6)]
        dense = l % 2 == 0
        li = l // 2

        def mixer(x, tok, mods):
            sh1, sc1, g1 = mods[0], mods[1], mods[2]
            u = _norm_mod(x, norm1_g3, l, sc1, sh1, tok)
            q = _proj(u, w_in, l, 0, att_w, tok, out_dtype=_BF16, norm_g3=q_norm_g3, scale=sm_scale,
                      name="proj_q")
            k = _proj(u, w_in, l, att_w, att_w, tok, out_dtype=_F32, norm_g3=k_norm_g3, name="proj_k")
            v = _proj(u, w_in, l, 2 * att_w, att_w, tok, out_dtype=_F32, name="proj_v")
            r = _proj(u, w_in, l, 3 * att_w, 2 * lru_w, tok, out_dtype=_F32, name="proj_lru")
            gates = _proj(u, gate_w, l, 0, 2 * d, tok, out_dtype=_F32, bias3=gate_b3, act="sigmoid",
                          name="proj_gates")
            return g1, q, k, v, r, gates

        def channel(x1s, toks, modss):
            if dense:
                res = []
                for x1, tok, mods in zip(x1s, toks, modss):
                    u2 = _norm_mod(x1, norm2_g3, l, mods[4], mods[3], tok)
                    res.append(_dense_ffn(u2, ffn_w1, ffn_w3, ffn_w2, li, x1, mods[5], tok))
                return res
            us, routes = [], []
            for x1, tok, mods in zip(x1s, toks, modss):
                u2, route = _norm_mod(x1, norm2_g3, l, mods[4], mods[3], tok, router=(router_w, router_b3, li))
                us.append(u2)
                routes.append(route)
            return _moe_ffn(us, routes, moe_w1, moe_w3, moe_w2, li, x1s, [m_[5] for m_ in modss], toks)

        g1, q, k, v, r, gates = mixer(xp, ptok, mods_p)
        o = _prompt_attention(q.reshape(bp, seq, att_w), k.reshape(bp, seq, att_w), v.reshape(bp, seq, att_w),
                              rel_bias, flash_bias, lam_par, subln_g3, l, lam_init, n_heads, t_att)
        r3 = r.reshape(bp, seq, 2 * lru_w)
        y, h_new = _lru_seq(r3, 0, lru_w, lru_w, zeros_conv, zeros_h, conv_w, conv_b3, lru_wr, lru_wi,
                            lru_br3, lru_bi3, lam3, l, _pick(seq, 256))
        conv_new = jnp.concatenate([zeros_conv, r3[:, -(_CONV_WIDTH - 1):, :lru_w]], axis=1)[:, -(_CONV_WIDTH - 1):]
        t_mix = _merge(o.reshape(bp * seq, att_w), y.reshape(bp * seq, lru_w), w_att, w_lru, gates, l, ptok)
        x1_p = _residual_mm(t_mix, w_o, (l,), xp, g1, ptok, name="out_proj")
        outs["kp"].append(k.reshape(bp, seq, n_heads, dv))
        outs["vp"].append(v.reshape(bp, seq, n_heads, dv))
        outs["cp"].append(conv_new)
        outs["hp"].append(h_new)

        g1, q, k, v, r, gates = mixer(xs, stok, mods_s)
        o = _sample_attention(q.reshape(bd, n_heads, dv), k.reshape(bd, n_heads, dv), v.reshape(bd, n_heads, dv),
                              cache_k, cache_v, page_table, dec_bias, lam_par, subln_g3, l, lam_init)
        conv_prev = state_conv[l]
        y, h_new = _lru_step(r, 0, lru_w, lru_w, jnp.swapaxes(conv_prev, 0, 1), state_h[l], conv_w, conv_b3,
                             lru_wr, lru_wi, lru_br3, lru_bi3, lam3, l)
        conv_new = jnp.concatenate([conv_prev, r[:, None, :lru_w]], axis=1)[:, -(_CONV_WIDTH - 1):]
        t_mix = _merge(o.reshape(bd, att_w), y, w_att, w_lru, gates, l, stok)
        x1_s = _residual_mm(t_mix, w_o, (l,), xs, g1, stok, name="out_proj")
        outs["ks"].append(k.reshape(bd, 1, n_heads, dv))
        outs["vs"].append(v.reshape(bd, 1, n_heads, dv))
        outs["cs"].append(conv_new)
        outs["hs"].append(h_new)

        xp, xs = channel([x1_p, x1_s], [ptok, stok], [mods_p, mods_s])

    st = jnp.stack
    return (xp.reshape(bp, seq, d), xs.reshape(bd, 1, d), st(outs["kp"]), st(outs["vp"]), st(outs["cp"]),
            st(outs["hp"]), st(outs["ks"]), st(outs["vs"]), st(outs["cs"]), st(outs["hs"]))
```
